```python
import jax
import jax.numpy as jnp
from jax import lax
import numpy as np

D_MODEL = 1024
BATCH = 4
SEQ = 8192
DEPTH = 4

CHUNK = 64
D_MIX = D_MODEL
HEAD_DIM = 64
GROUP_WIDTH = D_MIX // 4
N_GROUP_HEADS = GROUP_WIDTH // HEAD_DIM
ATTN_LEFT_CHUNKS = 8
ATTN_BAND = ATTN_LEFT_CHUNKS + 1
ATTN_MAX_REL = 128
LRU_CONV = 4
LRU_C = 8.0
RWKV_DECAY_RANK = 32
RWKV_ICLR_RANK = 32
RWKV_GATE_RANK = 64
RWKV_LN_EPS = 64e-5
SB_BLOCK = 128
D_FF = 2816
FFN_CONV = 3
NORM_EPS = 1e-6

P_ATTN = 3 * GROUP_WIDTH
P_LRU = 2 * GROUP_WIDTH
P_RWKV = 3 * GROUP_WIDTH + RWKV_DECAY_RANK + RWKV_ICLR_RANK + RWKV_GATE_RANK
P_SB = 3 * GROUP_WIDTH
P_TOTAL = P_ATTN + P_LRU + P_RWKV + P_SB

kernel_name = "hybrid_chunk_causal_encoder_trunk"


def rms_norm(x, g, eps=NORM_EPS):
    x32 = x.astype(jnp.float32)
    y = x32 * lax.rsqrt(jnp.mean(x32 * x32, axis=-1, keepdims=True) + eps)
    return (y * g.astype(jnp.float32)).astype(x.dtype)


def split_heads(t):
    return t.reshape(t.shape[:-1] + (-1, HEAD_DIM))


def causal_depthwise_conv(x, w, b):
    k = w.shape[0]
    y = lax.conv_general_dilated(
        x, w[:, None, :].astype(x.dtype), window_strides=(1,), padding=[(k - 1, 0)],
        dimension_numbers=("NWC", "WIO", "NWC"), feature_group_count=x.shape[-1])
    return y + b


def chunk_attention(q, k, v, q_gain, k_gain, rel_bias):
    bsz, seq, nh, dh = q.shape
    nc = seq // CHUNK
    band = ATTN_BAND * CHUNK
    q = rms_norm(q, q_gain).reshape(bsz, nc, CHUNK, nh, dh)
    k = rms_norm(k, k_gain)
    left = ((0, 0), (ATTN_LEFT_CHUNKS * CHUNK, 0), (0, 0), (0, 0))
    kp = jnp.pad(k, left).reshape(bsz, nc + ATTN_LEFT_CHUNKS, CHUNK, nh, dh)
    vp = jnp.pad(v, left).reshape(bsz, nc + ATTN_LEFT_CHUNKS, CHUNK, nh, dh)
    kb = jnp.concatenate([kp[:, i:i + nc] for i in range(ATTN_BAND)], axis=2)
    vb = jnp.concatenate([vp[:, i:i + nc] for i in range(ATTN_BAND)], axis=2)
    s = jnp.einsum("bnqhd,bnkhd->bnhqk", q.astype(jnp.float32), kb.astype(jnp.float32)) * (dh ** -0.5)
    dist = ATTN_LEFT_CHUNKS * CHUNK + jnp.arange(CHUNK)[:, None] - jnp.arange(band)[None, :]
    rel = jnp.clip(dist, -ATTN_MAX_REL, ATTN_MAX_REL) + ATTN_MAX_REL
    s = s + rel_bias.astype(jnp.float32)[:, rel]
    valid = ((jnp.arange(nc)[:, None] - ATTN_LEFT_CHUNKS) * CHUNK + jnp.arange(band)[None, :]) >= 0
    s = jnp.where(valid[None, :, None, None, :], s, -jnp.inf)
    p = jax.nn.softmax(s, axis=-1).astype(v.dtype)
    o = jnp.einsum("bnhqk,bnkhd->bnqhd", p, vb)
    return o.reshape(bsz, seq, nh * dh)


def _linear_combine(left, right):
    a_l, b_l = left
    a_r, b_r = right
    return a_l * a_r, a_r * b_l + b_r


def rglru_mixer(xb, gate, conv_w, conv_b, ra_w, ra_b, ri_w, ri_b, lam):
    bsz, seq, width = xb.shape
    xb = causal_depthwise_conv(xb, conv_w, conv_b)
    xh = xb.reshape(bsz, seq, N_GROUP_HEADS, HEAD_DIM)
    r_gate = jax.nn.sigmoid(jnp.einsum("bshi,hij->bshj", xh, ra_w).reshape(bsz, seq, width) + ra_b)
    i_gate = jax.nn.sigmoid(jnp.einsum("bshi,hij->bshj", xh, ri_w).reshape(bsz, seq, width) + ri_b)
    log_a = (-LRU_C * r_gate.astype(jnp.float32)) * jax.nn.softplus(-lam.astype(jnp.float32))
    a = jnp.exp(log_a)
    u = jnp.sqrt(-jnp.expm1(2.0 * log_a)) * (i_gate * xb).astype(jnp.float32)
    _, h = lax.associative_scan(_linear_combine, (a, u), axis=1)
    return h.astype(xb.dtype) * jax.nn.gelu(gate)


def rwkv7_mixer(p, mu, w0, w2, a0, a2, g2, k_k, k_a, r_k, ln_w, ln_b):
    bsz, seq, _ = p.shape
    gw, nh, n = GROUP_WIDTH, N_GROUP_HEADS, HEAD_DIM
    p_prev = jnp.pad(p, ((0, 0), (1, 0), (0, 0)))[:, :-1]
    p = p + (p_prev - p) * mu
    r, k, v = p[..., :gw], p[..., gw:2 * gw], p[..., 2 * gw:3 * gw]
    o1 = 3 * gw
    o2 = o1 + RWKV_DECAY_RANK
    o3 = o2 + RWKV_ICLR_RANK
    xw, xa, xg = p[..., o1:o2], p[..., o2:o3], p[..., o3:]
    w = -jax.nn.softplus(-(w0 + jnp.tanh(xw) @ w2)) - 0.5
    decay = jnp.exp(-jnp.exp(w.astype(jnp.float32)))
    a = jax.nn.sigmoid(a0 + xa @ a2)
    g = jax.nn.sigmoid(xg) @ g2
    kk = split_heads(k * k_k).astype(jnp.float32)
    kk = kk / jnp.maximum(jnp.linalg.norm(kk, axis=-1, keepdims=True), 1e-12)
    k = k * (1.0 + (a - 1.0) * k_a)
    rh = split_heads(r).astype(jnp.float32)
    kh = split_heads(k).astype(jnp.float32)
    vh = split_heads(v).astype(jnp.float32)
    ah = split_heads(a).astype(jnp.float32)
    dech = split_heads(decay)

    def time_major(t):
        return jnp.moveaxis(t, 1, 0)

    def step(state, inp):
        r_t, d_t, k_t, v_t, kk_t, a_t = inp
        sa = jnp.einsum("bhvk,bhk->bhv", state, -kk_t)
        state = (state * d_t[:, :, None, :] + sa[..., None] * (kk_t * a_t)[:, :, None, :]
                 + v_t[..., None] * k_t[:, :, None, :])
        return state, jnp.einsum("bhvk,bhk->bhv", state, r_t)

    state0 = jnp.zeros((bsz, nh, n, n), jnp.float32)
    _, y = lax.scan(step, state0, (time_major(rh), time_major(dech), time_major(kh),
                                   time_major(vh), time_major(kk), time_major(ah)))
    y = jnp.moveaxis(y, 0, 1)
    mean = jnp.mean(y, axis=-1, keepdims=True)
    var = jnp.mean(jnp.square(y - mean), axis=-1, keepdims=True)
    y = ((y - mean) * lax.rsqrt(var + RWKV_LN_EPS) * ln_w.astype(jnp.float32).reshape(nh, n)
         + ln_b.astype(jnp.float32).reshape(nh, n))
    bonus = jnp.sum(rh * kh * r_k.astype(jnp.float32).reshape(nh, n), axis=-1, keepdims=True) * vh
    y = (y + bonus).reshape(bsz, seq, gw).astype(p.dtype)
    return y * g


def stick_breaking(q, k, v):
    bsz, seq, nh, dh = q.shape
    nb = seq // SB_BLOCK
    k32 = k.astype(jnp.float32)
    v32 = v.astype(jnp.float32)
    kpos = jnp.arange(seq)
    qb = jnp.moveaxis(q.reshape(bsz, nb, SB_BLOCK, nh, dh), 1, 0)

    def block(args):
        q_blk, bi = args
        z = jnp.einsum("bqhd,bkhd->bhqk", q_blk.astype(jnp.float32), k32) * (dh ** -0.5)
        qpos = bi * SB_BLOCK + jnp.arange(SB_BLOCK)
        causal = kpos[None, :] < qpos[:, None]
        log_keep = jnp.where(causal, jax.nn.log_sigmoid(-z), 0.0)
        between = lax.cumsum(log_keep, axis=3, reverse=True) - log_keep
        weight = jnp.where(causal, jnp.exp(jax.nn.log_sigmoid(z) + between), 0.0)
        return jnp.einsum("bhqk,bkhd->bqhd", weight, v32)

    out = lax.map(block, (qb, jnp.arange(nb)))
    return jnp.moveaxis(out, 0, 1).reshape(bsz, seq, nh * dh).astype(q.dtype)


def conv_ffn(h, w_up, conv_w, conv_b, w_down):
    u = causal_depthwise_conv(h @ w_up, conv_w, conv_b)
    val, gate = jnp.split(u, 2, axis=-1)
    return (val * jax.nn.gelu(gate)) @ w_down


def setup_inputs(seed: int = 0) -> dict:
    key = jax.random.key(seed)
    keys = iter(jax.random.split(key, 48))
    f32 = jnp.float32
    L, D, G, H, N, F = DEPTH, D_MODEL, GROUP_WIDTH, N_GROUP_HEADS, HEAD_DIM, D_FF

    def normal(shape, scale):
        return jax.random.normal(next(keys), shape, f32) * scale

    def gain(shape, center=1.0):
        return center + 0.02 * jax.random.normal(next(keys), shape, f32)

    lru_a = jax.random.uniform(next(keys), (L, G), f32, 0.9, 0.999) ** (1.0 / LRU_C)
    return {
        "x": normal((BATCH, SEQ, D), 1.0),
        "c": normal((BATCH, D), 1.0),
        "ada_w": normal((L, D, 6 * D), 0.5 * D ** -0.5),
        "ada_b": normal((L, 6 * D), 0.02),
        "norm1_g": gain((L, D)),
        "norm2_g": gain((L, D)),
        "w_in": normal((L, D, P_TOTAL), D ** -0.5),
        "w_out": normal((L, D_MIX, D), D_MIX ** -0.5),
        "attn_q_gain": gain((L, N)),
        "attn_k_gain": gain((L, N)),
        "attn_rel_bias": normal((L, H, 2 * ATTN_MAX_REL + 1), 0.5),
        "lru_conv_w": normal((L, LRU_CONV, G), LRU_CONV ** -0.5),
        "lru_conv_b": normal((L, G), 0.02),
        "lru_ra_w": normal((L, H, N, N), N ** -0.5),
        "lru_ra_b": normal((L, G), 0.02),
        "lru_ri_w": normal((L, H, N, N), N ** -0.5),
        "lru_ri_b": normal((L, G), 0.02),
        "lru_lambda": jnp.log(lru_a) - jnp.log1p(-lru_a),
        "rwkv_mu": jax.random.uniform(next(keys), (L, P_RWKV), f32),
        "rwkv_w0": jax.random.uniform(next(keys), (L, G), f32, -6.0, -1.0),
        "rwkv_w2": normal((L, RWKV_DECAY_RANK, G), 0.5 * RWKV_DECAY_RANK ** -0.5),
        "rwkv_a0": normal((L, G), 0.1),
        "rwkv_a2": normal((L, RWKV_ICLR_RANK, G), RWKV_ICLR_RANK ** -0.5),
        "rwkv_g2": normal((L, RWKV_GATE_RANK, G), RWKV_GATE_RANK ** -0.5),
        "rwkv_k_k": gain((L, G), 0.85),
        "rwkv_k_a": gain((L, G)),
        "rwkv_r_k": normal((L, G), 0.1),
        "rwkv_ln_w": gain((L, G)),
        "rwkv_ln_b": normal((L, G), 0.02),
        "ffn_up": normal((L, D, 2 * F), D ** -0.5),
        "ffn_conv_w": normal((L, FFN_CONV, 2 * F), FFN_CONV ** -0.5),
        "ffn_conv_b": normal((L, 2 * F), 0.02),
        "ffn_down": normal((L, F, D), F ** -0.5),
    }


def reference(x, c, ada_w, ada_b, norm1_g, norm2_g, w_in, w_out,
              attn_q_gain, attn_k_gain, attn_rel_bias,
              lru_conv_w, lru_conv_b, lru_ra_w, lru_ra_b, lru_ri_w, lru_ri_b, lru_lambda,
              rwkv_mu, rwkv_w0, rwkv_w2, rwkv_a0, rwkv_a2, rwkv_g2, rwkv_k_k, rwkv_k_a,
              rwkv_r_k, rwkv_ln_w, rwkv_ln_b,
              ffn_up, ffn_conv_w, ffn_conv_b, ffn_down):
    cond = jax.nn.silu(c)
    cuts = [P_ATTN, P_ATTN + P_LRU, P_ATTN + P_LRU + P_RWKV]
    for l in range(DEPTH):
        mod = (cond @ ada_w[l] + ada_b[l])[:, None, :]
        sh_m, sc_m, gt_m, sh_f, sc_f, gt_f = jnp.split(mod, 6, axis=-1)
        h = rms_norm(x, norm1_g[l]) * (1.0 + sc_m) + sh_m
        proj = h @ w_in[l]
        pa, pb, pc, pd = jnp.split(proj, cuts, axis=-1)
        qa, ka, va = jnp.split(pa, 3, axis=-1)
        y_a = chunk_attention(split_heads(qa), split_heads(ka), split_heads(va),
                              attn_q_gain[l], attn_k_gain[l], attn_rel_bias[l])
        xb, gb = jnp.split(pb, 2, axis=-1)
        y_b = rglru_mixer(xb, gb, lru_conv_w[l], lru_conv_b[l], lru_ra_w[l], lru_ra_b[l],
                          lru_ri_w[l], lru_ri_b[l], lru_lambda[l])
        y_c = rwkv7_mixer(pc, rwkv_mu[l], rwkv_w0[l], rwkv_w2[l], rwkv_a0[l], rwkv_a2[l],
                          rwkv_g2[l], rwkv_k_k[l], rwkv_k_a[l], rwkv_r_k[l],
                          rwkv_ln_w[l], rwkv_ln_b[l])
        qd, kd, vd = jnp.split(pd, 3, axis=-1)
        y_d = stick_breaking(split_heads(qd), split_heads(kd), split_heads(vd))
        mixed = jnp.concatenate([y_a, y_b, y_c, y_d], axis=-1) @ w_out[l]
        x = x + gt_m * mixed
        h = rms_norm(x, norm2_g[l]) * (1.0 + sc_f) + sh_f
        x = x + gt_f * conv_ffn(h, ffn_up[l], ffn_conv_w[l], ffn_conv_b[l], ffn_down[l])
    return x
```

```python
import functools

import jax
import jax.numpy as jnp
from jax import lax
from jax.experimental import pallas as pl
from jax.experimental.pallas import tpu as pltpu

F32 = jnp.float32
BF16 = jnp.bfloat16
HI = lax.Precision.HIGHEST

HEAD_DIM = 64
N_HEADS = 4
GROUP = HEAD_DIM * N_HEADS
CHUNK = 64
ATTN_LEFT_CHUNKS = 8
ATTN_MAX_REL = 128
LRU_CONV = 4
LRU_C = 8.0
RWKV_LN_EPS = 64e-5
FFN_CONV = 3
NORM_EPS = 1e-6
HALO = 8
NEG_BIG = -1e30

VMEM_LIMIT = 56 * 1024 * 1024

ATTN_QB = 512
SEQ_TILE = 512
SB_BLOCK = 128
RWKV_CHUNK = 64
FFN_CHUNK = 256


def _params(*sem):
    return pltpu.CompilerParams(dimension_semantics=sem, vmem_limit_bytes=VMEM_LIMIT)


def _sigmoid(x):
    return 1.0 / (1.0 + jnp.exp(-x))


def _softplus(x):
    return jnp.maximum(x, 0.0) + jnp.log1p(jnp.exp(-jnp.abs(x)))


def _gelu_tanh(x):
    return 0.5 * x * (1.0 + jnp.tanh(0.7978845608028654 * (x + 0.044715 * x * x * x)))


def _dot_nt(a, b, **kw):
    return lax.dot_general(a, b, (((1,), (1,)), ((), ())), preferred_element_type=F32, **kw)


def _dot_tn(a, b, **kw):
    return lax.dot_general(a, b, (((0,), (0,)), ((), ())), preferred_element_type=F32, **kw)


def _dot(a, b, **kw):
    return jnp.dot(a, b, preferred_element_type=F32, **kw)


def _const_spec(shape):
    nd = len(shape)
    return pl.BlockSpec(shape, lambda *_: (0,) * nd)


def _ada_kernel(c_ref, w_ref, b_ref, o_ref):
    c = c_ref[...]
    s = c * _sigmoid(c)
    o_ref[0] = _dot(s, w_ref[0], precision=HI) + b_ref[0]


def ada_modulation(c, ada_w, ada_b):
    nl, d, n = ada_w.shape
    b = c.shape[0]
    rows = 8
    cp = jnp.zeros((rows, d), F32).at[:b].set(c)
    tn = 1536
    out = pl.pallas_call(
        _ada_kernel,
        grid=(nl, n // tn),
        in_specs=[
            pl.BlockSpec((rows, d), lambda l, j: (0, 0)),
            pl.BlockSpec((1, d, tn), lambda l, j: (l, 0, j)),
            pl.BlockSpec((1, 1, tn), lambda l, j: (l, 0, j)),
        ],
        out_specs=pl.BlockSpec((1, rows, tn), lambda l, j: (l, 0, j)),
        out_shape=jax.ShapeDtypeStruct((nl, rows, n), F32),
        compiler_params=_params("arbitrary", "arbitrary"),
        name="ada_modulation",
    )(cp, ada_w, ada_b.reshape(nl, 1, n))
    return out[:, :b]


def _modulated_norm(x, g, sc, sh):
    ms = jnp.mean(x * x, axis=-1, keepdims=True)
    return (x * lax.rsqrt(ms + NORM_EPS) * g) * (1.0 + sc) + sh


def _inproj_kernel(x_ref, sc_ref, sh_ref, g_ref, w_ref, oa_ref, ob_ref, oc_ref, od_ref, *, cuts):
    h = _modulated_norm(x_ref[0], g_ref[...], sc_ref[0], sh_ref[0]).astype(BF16)
    c0, c1, c2, c3 = cuts
    oa_ref[0] = _dot(h, w_ref[:, 0:c0])
    ob_ref[0] = _dot(h, w_ref[:, c0:c1])
    oc_ref[0] = _dot(h, w_ref[:, c1:c2])
    od_ref[0] = _dot(h, w_ref[:, c2:c3]).astype(BF16)


def in_projection(x, sc, sh, g, w_bf16, widths, tile):
    b, s, d = x.shape
    cuts = tuple(int(sum(widths[:i + 1])) for i in range(4))
    row = lambda bi, i: (bi, i, 0)
    vec = lambda bi, i: (bi, 0, 0)
    dts = (F32, F32, F32, BF16)
    return pl.pallas_call(
        functools.partial(_inproj_kernel, cuts=cuts),
        grid=(b, s // tile),
        in_specs=[
            pl.BlockSpec((1, tile, d), row),
            pl.BlockSpec((1, 1, d), vec),
            pl.BlockSpec((1, 1, d), vec),
            _const_spec((1, d)),
            _const_spec(w_bf16.shape),
        ],
        out_specs=[pl.BlockSpec((1, tile, w), row) for w in widths],
        out_shape=[jax.ShapeDtypeStruct((b, s, w), dt) for w, dt in zip(widths, dts)],
        compiler_params=_params("arbitrary", "arbitrary"),
        name="in_projection",
    )(x, sc, sh, g.reshape(1, d), w_bf16)


def _head_rms(x, gain):
    ms = jnp.mean(x * x, axis=-1, keepdims=True)
    return x * lax.rsqrt(ms + NORM_EPS) * gain


def _attn_kernel(q_ref, kp_ref, kc_ref, vp_ref, vc_ref, qg_ref, kg_ref, bias_ref, o_ref):
    i = pl.program_id(1)
    qb = q_ref.shape[1]
    col = lax.broadcasted_iota(jnp.int32, (qb, 2 * qb), 1)
    dead = jnp.logical_and(i == 0, col < qb)
    for h in range(N_HEADS):
        lanes = slice(h * HEAD_DIM, (h + 1) * HEAD_DIM)
        q = _head_rms(q_ref[0, :, lanes], qg_ref[...]).astype(BF16)
        k = jnp.concatenate([kp_ref[0, :, lanes], kc_ref[0, :, lanes]], axis=0)
        k = _head_rms(k, kg_ref[...]).astype(BF16)
        v = jnp.concatenate([vp_ref[0, :, lanes], vc_ref[0, :, lanes]], axis=0).astype(BF16)
        s = _dot_nt(q, k) * (HEAD_DIM ** -0.5) + bias_ref[h]
        s = jnp.where(dead, NEG_BIG, s)
        m = jnp.max(s, axis=-1, keepdims=True)
        p = jnp.exp(s - m)
        den = jnp.sum(p, axis=-1, keepdims=True)
        o_ref[0, :, lanes] = _dot(p.astype(BF16), v) / den


def attention_bias(rel_bias, qb):
    qi = jnp.arange(qb)[:, None]
    kj = jnp.arange(2 * qb)[None, :]
    dist = qi + qb - kj
    rel = jnp.clip(dist, -ATTN_MAX_REL, ATTN_MAX_REL) + ATTN_MAX_REL
    qc = qi // CHUNK + qb // CHUNK
    kc = kj // CHUNK
    valid = jnp.logical_and(kc <= qc, kc >= qc - ATTN_LEFT_CHUNKS)
    return jnp.where(valid[None], rel_bias.astype(F32)[:, rel], NEG_BIG)


def chunk_attention(pa, q_gain, k_gain, bias, qb):
    b, s, _ = pa.shape
    blk = (1, qb, GROUP)
    prev = lambda bi, i, c: (bi, jnp.maximum(i - 1, 0), c)
    return pl.pallas_call(
        _attn_kernel,
        grid=(b, s // qb),
        in_specs=[
            pl.BlockSpec(blk, lambda bi, i: (bi, i, 0)),
            pl.BlockSpec(blk, lambda bi, i: prev(bi, i, 1)),
            pl.BlockSpec(blk, lambda bi, i: (bi, i, 1)),
            pl.BlockSpec(blk, lambda bi, i: prev(bi, i, 2)),
            pl.BlockSpec(blk, lambda bi, i: (bi, i, 2)),
            _const_spec((1, HEAD_DIM)),
            _const_spec((1, HEAD_DIM)),
            _const_spec(bias.shape),
        ],
        out_specs=pl.BlockSpec(blk, lambda bi, i: (bi, i, 0)),
        out_shape=jax.ShapeDtypeStruct((b, s, GROUP), F32),
        compiler_params=_params("arbitrary", "arbitrary"),
        name="chunk_attention",
    )(pa, pa, pa, pa, pa, q_gain.reshape(1, HEAD_DIM), k_gain.reshape(1, HEAD_DIM), bias)


def _lru_kernel(p_ref, cw_ref, cb_ref, raw_ref, rab_ref, riw_ref, rib_ref, lam_ref, o_ref,
                xpad_ref, h_ref):
    i = pl.program_id(1)
    t = p_ref.shape[1]

    @pl.when(i == 0)
    def _():
        xpad_ref[0:HALO, :] = jnp.zeros((HALO, GROUP), F32)
        h_ref[...] = jnp.zeros_like(h_ref)

    @pl.when(i > 0)
    def _():
        xpad_ref[0:HALO, :] = xpad_ref[t:t + HALO, :]

    xpad_ref[HALO:HALO + t, :] = p_ref[0, :, 0:GROUP]
    gate = p_ref[0, :, GROUP:2 * GROUP]
    xc = cb_ref[...]
    for j in range(LRU_CONV):
        xc = xc + cw_ref[j:j + 1, :] * xpad_ref[pl.ds(HALO - (LRU_CONV - 1) + j, t), :]
    xcb = xc.astype(BF16)
    r_gate = _sigmoid(_dot(xcb, raw_ref[...]) + rab_ref[...])
    i_gate = _sigmoid(_dot(xcb, riw_ref[...]) + rib_ref[...])
    log_a = (-LRU_C * r_gate) * _softplus(-lam_ref[...])
    a = jnp.exp(log_a)
    u = jnp.sqrt(-jnp.tanh(log_a) * (a * a + 1.0)) * (i_gate * xc)
    rows = lax.broadcasted_iota(jnp.int32, (t, 1), 0)
    k = 1
    while k < t:
        keep = rows >= k
        a_sh = jnp.where(keep, pltpu.roll(a, k, 0), 1.0)
        u_sh = jnp.where(keep, pltpu.roll(u, k, 0), 0.0)
        u = a * u_sh + u
        a = a * a_sh
        k *= 2
    hh = a * h_ref[...] + u
    h_ref[...] = hh[t - 1:t, :]
    o_ref[0] = hh * _gelu_tanh(gate)


def _block_diag(w):
    h, n, _ = w.shape
    eye = jnp.eye(h, dtype=w.dtype)
    return (eye[:, None, :, None] * w[:, :, None, :]).reshape(h * n, h * n)


def rglru(pb, conv_w, conv_b, ra_w, ra_b, ri_w, ri_b, lam, tile):
    b, s, _ = pb.shape
    row = lambda bi, i: (bi, i, 0)
    vec = lambda a: a.reshape(1, GROUP)
    return pl.pallas_call(
        _lru_kernel,
        grid=(b, s // tile),
        in_specs=[
            pl.BlockSpec((1, tile, 2 * GROUP), row),
            _const_spec((LRU_CONV, GROUP)),
            _const_spec((1, GROUP)),
            _const_spec((GROUP, GROUP)),
            _const_spec((1, GROUP)),
            _const_spec((GROUP, GROUP)),
            _const_spec((1, GROUP)),
            _const_spec((1, GROUP)),
        ],
        out_specs=pl.BlockSpec((1, tile, GROUP), row),
        out_shape=jax.ShapeDtypeStruct((b, s, GROUP), F32),
        scratch_shapes=[pltpu.VMEM((tile + HALO, GROUP), F32), pltpu.VMEM((1, GROUP), F32)],
        compiler_params=_params("arbitrary", "arbitrary"),
        name="rglru",
    )(pb, conv_w, vec(conv_b), _block_diag(ra_w).astype(BF16), vec(ra_b),
      _block_diag(ri_w).astype(BF16), vec(ri_b), vec(lam))


def _rwkv_kernel(p_ref, mu_ref, w0_ref, w2_ref, a0_ref, a2_ref, g2_ref, kk_ref, ka_ref, rk_ref,
                 lnw_ref, lnb_ref, o_ref,
                 ppad_ref, state_ref, logd_ref, r_ref, k_ref, v_ref, kk_s_ref, kka_ref, y_ref):
    i = pl.program_id(1)
    t = p_ref.shape[1]
    width = p_ref.shape[2]
    ch = RWKV_CHUNK
    g3 = 3 * GROUP

    @pl.when(i == 0)
    def _():
        ppad_ref[0:HALO, :] = jnp.zeros((HALO, width), F32)
        state_ref[...] = jnp.zeros_like(state_ref)

    @pl.when(i > 0)
    def _():
        ppad_ref[0:HALO, :] = ppad_ref[t:t + HALO, :]

    p = p_ref[0]
    ppad_ref[HALO:HALO + t, :] = p
    p_prev = ppad_ref[pl.ds(HALO - 1, t), :]
    p = p + (p_prev - p) * mu_ref[...]
    r = p[:, 0:GROUP]
    k = p[:, GROUP:2 * GROUP]
    v = p[:, 2 * GROUP:g3]
    low = p[:, g3:width]
    w = -_softplus(-(w0_ref[...] + _dot(jnp.tanh(low).astype(BF16), w2_ref[...]))) - 0.5
    a = _sigmoid(a0_ref[...] + _dot(low.astype(BF16), a2_ref[...]))
    g = _dot(_sigmoid(low).astype(BF16), g2_ref[...])

    hr = lax.broadcasted_iota(jnp.int32, (GROUP, GROUP), 0) // HEAD_DIM
    hc = lax.broadcasted_iota(jnp.int32, (GROUP, GROUP), 1) // HEAD_DIM
    head_ones = (hr == hc).astype(F32)

    def head_sum(x):
        return _dot(x, head_ones, precision=HI)

    kk = k * kk_ref[...]
    kk = kk / jnp.maximum(jnp.sqrt(head_sum(kk * kk)), 1e-12)
    k = k * (1.0 + (a - 1.0) * ka_ref[...])

    logd_ref[...] = -jnp.exp(w)
    r_ref[...] = r
    k_ref[...] = k
    v_ref[...] = v
    kk_s_ref[...] = kk
    kka_ref[...] = kk * a

    rr = lax.broadcasted_iota(jnp.int32, (ch, ch), 0)
    cc = lax.broadcasted_iota(jnp.int32, (ch, ch), 1)
    lower_incl = rr >= cc
    lower_strict = rr > cc
    tril = lower_incl.astype(F32)
    eye = (rr == cc).astype(F32)

    def chunk_body(c, carry):
        rows = pl.ds(pl.multiple_of(c * ch, ch), ch)
        logd = logd_ref[rows, :]
        cum = _dot(tril, logd, precision=HI)
        cum_end = cum[ch - 1:ch, :]
        p_in = jnp.exp(cum)
        p_inv = jnp.exp(-cum)
        p_end = jnp.exp(cum_end - cum)
        kk_c = kk_s_ref[rows, :]
        kka_c = kka_ref[rows, :]
        k_c = k_ref[rows, :]
        a_t = -kk_c * jnp.exp(cum - logd)
        b_t = kka_c * p_inv
        k_t = k_c * p_inv
        r_t = r_ref[rows, :] * p_in
        b_h = kka_c * p_end
        k_h = k_c * p_end
        decay_end = jnp.exp(cum_end)
        v_c = v_ref[rows, :]
        for h in range(N_HEADS):
            ln = slice(h * HEAD_DIM, (h + 1) * HEAD_DIM)
            ah, bh, kh, rh, vh = a_t[:, ln], b_t[:, ln], k_t[:, ln], r_t[:, ln], v_c[:, ln]
            n = jnp.where(lower_strict, _dot_nt(ah, bh, precision=HI), 0.0)
            a_ak = jnp.where(lower_strict, _dot_nt(ah, kh, precision=HI), 0.0)
            a_rb = jnp.where(lower_incl, _dot_nt(rh, bh, precision=HI), 0.0)
            a_rk = jnp.where(lower_incl, _dot_nt(rh, kh, precision=HI), 0.0)
            tinv = eye + n
            npow = n
            steps = 1
            while 2 * steps < ch:
                npow = _dot(npow, npow, precision=HI)
                tinv = tinv + _dot(tinv, npow, precision=HI)
                steps *= 2
            st = state_ref[h]
            rhs = _dot_nt(ah, st, precision=HI) + _dot(a_ak, vh, precision=HI)
            u = _dot(tinv, rhs, precision=HI)
            y = _dot_nt(rh, st, precision=HI) + _dot(a_rb, u, precision=HI) + _dot(a_rk, vh, precision=HI)
            state_ref[h] = (st * decay_end[:, ln] + _dot_tn(u, b_h[:, ln], precision=HI)
                            + _dot_tn(vh, k_h[:, ln], precision=HI))
            y_ref[rows, ln] = y
        return carry

    lax.fori_loop(0, t // ch, chunk_body, 0)

    y = y_ref[...]
    mean = head_sum(y) * (1.0 / HEAD_DIM)
    yc = y - mean
    var = head_sum(yc * yc) * (1.0 / HEAD_DIM)
    y = yc * lax.rsqrt(var + RWKV_LN_EPS) * lnw_ref[...] + lnb_ref[...]
    r = r_ref[...]
    k = k_ref[...]
    v = v_ref[...]
    y = y + head_sum(r * k * rk_ref[...]) * v
    o_ref[0] = y * g


def _pad_rows(w, start, total):
    return jnp.zeros((total, w.shape[1]), w.dtype).at[start:start + w.shape[0]].set(w)


def rwkv7(pc, mu, w0, w2, a0, a2, g2, k_k, k_a, r_k, ln_w, ln_b, tile):
    b, s, width = pc.shape
    low = width - 3 * GROUP
    dr, ir = w2.shape[0], a2.shape[0]
    row = lambda bi, i: (bi, i, 0)
    vec = lambda a: a.reshape(1, -1)
    tile_buf = pltpu.VMEM((tile, GROUP), F32)
    return pl.pallas_call(
        _rwkv_kernel,
        grid=(b, s // tile),
        in_specs=[
            pl.BlockSpec((1, tile, width), row),
            _const_spec((1, width)),
            _const_spec((1, GROUP)),
            _const_spec((low, GROUP)),
            _const_spec((1, GROUP)),
            _const_spec((low, GROUP)),
            _const_spec((low, GROUP)),
            _const_spec((1, GROUP)),
            _const_spec((1, GROUP)),
            _const_spec((1, GROUP)),
            _const_spec((1, GROUP)),
            _const_spec((1, GROUP)),
        ],
        out_specs=pl.BlockSpec((1, tile, GROUP), row),
        out_shape=jax.ShapeDtypeStruct((b, s, GROUP), F32),
        scratch_shapes=[
            pltpu.VMEM((tile + HALO, width), F32),
            pltpu.VMEM((N_HEADS, HEAD_DIM, HEAD_DIM), F32),
            tile_buf, tile_buf, tile_buf, tile_buf, tile_buf, tile_buf, tile_buf,
        ],
        compiler_params=_params("arbitrary", "arbitrary"),
        name="rwkv7",
    )(pc, vec(mu), vec(w0), _pad_rows(w2, 0, low).astype(BF16), vec(a0),
      _pad_rows(a2, dr, low).astype(BF16), _pad_rows(g2, dr + ir, low).astype(BF16),
      vec(k_k), vec(k_a), vec(r_k), vec(ln_w), vec(ln_b))


def _sb_kernel(q_ref, kv_ref, o_ref):
    qi = pl.program_id(1)
    blk = q_ref.shape[1]
    rr = lax.broadcasted_iota(jnp.int32, (blk, blk), 0)
    cc = lax.broadcasted_iota(jnp.int32, (blk, blk), 1)
    suffix = (rr >= cc).astype(BF16)
    causal = cc < rr

    for h in range(N_HEADS):
        q = q_ref[0, :, h * HEAD_DIM:(h + 1) * HEAD_DIM]

        def tile(kb, diagonal, h=h, q=q):
            rows = pl.ds(pl.multiple_of(kb * blk, blk), blk)
            k = kv_ref[0, rows, h * HEAD_DIM:(h + 1) * HEAD_DIM]
            v = kv_ref[0, rows, GROUP + h * HEAD_DIM:GROUP + (h + 1) * HEAD_DIM]
            z = _dot_nt(q, k) * (HEAD_DIM ** -0.5)
            lk = -_softplus(z)
            if diagonal:
                lk = jnp.where(causal, lk, 0.0)
            hi = lk.astype(BF16)
            lo = (lk - hi.astype(F32)).astype(BF16)
            incl = _dot(hi, suffix) + _dot(lo, suffix)
            return z, incl, v

        z, incl, v = tile(qi, True)
        wgt = jnp.exp(jnp.where(causal, z + incl, NEG_BIG))
        acc = _dot(wgt.astype(BF16), v)
        run = incl[:, 0:1]

        def body(j, carry, tile=tile):
            acc, run = carry
            z, incl, v = tile(qi - 1 - j, False)
            wgt = jnp.exp(z + incl + run)
            return acc + _dot(wgt.astype(BF16), v), run + incl[:, 0:1]

        acc, run = lax.fori_loop(0, qi, body, (acc, run))
        o_ref[0, :, h * HEAD_DIM:(h + 1) * HEAD_DIM] = acc


def stick_breaking(pd_bf16, blk):
    b, s, _ = pd_bf16.shape
    return pl.pallas_call(
        _sb_kernel,
        grid=(b, s // blk),
        in_specs=[
            pl.BlockSpec((1, blk, GROUP), lambda bi, i: (bi, i, 0)),
            pl.BlockSpec((1, s, 2 * GROUP), lambda bi, i: (bi, 0, 0)),
        ],
        out_specs=pl.BlockSpec((1, blk, GROUP), lambda bi, i: (bi, i, 0)),
        out_shape=jax.ShapeDtypeStruct((b, s, GROUP), F32),
        compiler_params=_params("arbitrary", "arbitrary"),
        name="stick_breaking",
    )(pd_bf16, pd_bf16[:, :, GROUP:])


def _outproj_kernel(x_ref, ya_ref, yb_ref, yc_ref, yd_ref, gt_ref, w_ref, o_ref):
    y = jnp.concatenate([ya_ref[0], yb_ref[0], yc_ref[0], yd_ref[0]], axis=-1).astype(BF16)
    o_ref[0] = x_ref[0] + gt_ref[0] * _dot(y, w_ref[...])


def out_projection(x, ys, gt, w_bf16, tile):
    b, s, d = x.shape
    row = lambda bi, i: (bi, i, 0)
    return pl.pallas_call(
        _outproj_kernel,
        grid=(b, s // tile),
        in_specs=[pl.BlockSpec((1, tile, d), row)]
        + [pl.BlockSpec((1, tile, GROUP), row) for _ in ys]
        + [pl.BlockSpec((1, 1, d), lambda bi, i: (bi, 0, 0)), _const_spec(w_bf16.shape)],
        out_specs=pl.BlockSpec((1, tile, d), row),
        out_shape=jax.ShapeDtypeStruct((b, s, d), F32),
        compiler_params=_params("arbitrary", "arbitrary"),
        name="out_projection",
    )(x, *ys, gt, w_bf16)


def _ffn_kernel(x_ref, xh_ref, sc_ref, sh_ref, gt_ref, g_ref, wv_ref, wg_ref, cwv_ref, cwg_ref,
                cbv_ref, cbg_ref, wd_ref, o_ref, uv_ref, ug_ref, acc_ref):
    i = pl.program_id(1)
    t = x_ref.shape[1]
    x = x_ref[0]
    xe = jnp.concatenate([xh_ref[0], x], axis=0)
    h = _modulated_norm(xe, g_ref[...], sc_ref[0], sh_ref[0]).astype(BF16)
    live = jnp.logical_or(i > 0, lax.broadcasted_iota(jnp.int32, (t + HALO, 1), 0) >= HALO)
    acc_ref[...] = jnp.zeros_like(acc_ref)

    def conv(u_ref, cw, cb):
        out = cb
        for j in range(FFN_CONV):
            out = out + cw[j:j + 1, :] * u_ref[pl.ds(HALO - (FFN_CONV - 1) + j, t), :]
        return out

    def body(f, carry):
        uv_ref[...] = jnp.where(live, _dot(h, wv_ref[f]), 0.0)
        ug_ref[...] = jnp.where(live, _dot(h, wg_ref[f]), 0.0)
        val = conv(uv_ref, cwv_ref[f], cbv_ref[f])
        gate = conv(ug_ref, cwg_ref[f], cbg_ref[f])
        acc_ref[...] += _dot((val * _gelu_tanh(gate)).astype(BF16), wd_ref[f])
        return carry

    lax.fori_loop(0, wv_ref.shape[0], body, 0)
    o_ref[0] = x + gt_ref[0] * acc_ref[...]


def conv_ffn(x, sc, sh, gt, g, w_up, conv_w, conv_b, w_down, tile, fchunk):
    b, s, d = x.shape
    f = w_down.shape[0]
    nf = f // fchunk
    row = lambda bi, i: (bi, i, 0)
    vec = lambda bi, i: (bi, 0, 0)
    halo = lambda bi, i: (bi, jnp.maximum(i * (tile // HALO) - 1, 0), 0)
    cols = lambda w: w.reshape(w.shape[0], nf, fchunk).transpose(1, 0, 2)
    wv = cols(w_up[:, :f]).astype(BF16)
    wg = cols(w_up[:, f:]).astype(BF16)
    wd = w_down.reshape(nf, fchunk, d).astype(BF16)
    cwv, cwg = cols(conv_w[:, :f]), cols(conv_w[:, f:])
    cbv, cbg = cols(conv_b[None, :f]), cols(conv_b[None, f:])
    return pl.pallas_call(
        _ffn_kernel,
        grid=(b, s // tile),
        in_specs=[
            pl.BlockSpec((1, tile, d), row),
            pl.BlockSpec((1, HALO, d), halo),
            pl.BlockSpec((1, 1, d), vec),
            pl.BlockSpec((1, 1, d), vec),
            pl.BlockSpec((1, 1, d), vec),
            _const_spec((1, d)),
            _const_spec(wv.shape),
            _const_spec(wg.shape),
            _const_spec(cwv.shape),
            _const_spec(cwg.shape),
            _const_spec(cbv.shape),
            _const_spec(cbg.shape),
            _const_spec(wd.shape),
        ],
        out_specs=pl.BlockSpec((1, tile, d), row),
        out_shape=jax.ShapeDtypeStruct((b, s, d), F32),
        scratch_shapes=[
            pltpu.VMEM((tile + HALO, fchunk), F32),
            pltpu.VMEM((tile + HALO, fchunk), F32),
            pltpu.VMEM((tile, d), F32),
        ],
        compiler_params=_params("arbitrary", "arbitrary"),
        name="conv_ffn",
    )(x, x, sc, sh, gt, g.reshape(1, d), wv, wg, cwv, cwg, cbv, cbg, wd)


def kernel(x, c, ada_w, ada_b, norm1_g, norm2_g, w_in, w_out, attn_q_gain, attn_k_gain, attn_rel_bias, lru_conv_w, lru_conv_b, lru_ra_w, lru_ra_b, lru_ri_w, lru_ri_b, lru_lambda, rwkv_mu, rwkv_w0, rwkv_w2, rwkv_a0, rwkv_a2, rwkv_g2, rwkv_k_k, rwkv_k_a, rwkv_r_k, rwkv_ln_w, rwkv_ln_b, ffn_up, ffn_conv_w, ffn_conv_b, ffn_down):
    depth = ada_w.shape[0]
    bsz, seq, d = x.shape
    p_rwkv = rwkv_mu.shape[1]
    widths = (3 * GROUP, 2 * GROUP, p_rwkv, 3 * GROUP)
    tile = min(SEQ_TILE, seq)
    qb = min(ATTN_QB, seq)
    mods = ada_modulation(c, ada_w, ada_b)
    for l in range(depth):
        sh_m, sc_m, gt_m, sh_f, sc_f, gt_f = [
            mods[l, :, j * d:(j + 1) * d].reshape(bsz, 1, d) for j in range(6)]
        pa, pb, pc, pd = in_projection(x, sc_m, sh_m, norm1_g[l], w_in[l].astype(BF16), widths, tile)
        y_a = chunk_attention(pa, attn_q_gain[l], attn_k_gain[l],
                              attention_bias(attn_rel_bias[l], qb), qb)
        y_b = rglru(pb, lru_conv_w[l], lru_conv_b[l], lru_ra_w[l], lru_ra_b[l],
                    lru_ri_w[l], lru_ri_b[l], lru_lambda[l], tile)
        y_c = rwkv7(pc, rwkv_mu[l], rwkv_w0[l], rwkv_w2[l], rwkv_a0[l], rwkv_a2[l], rwkv_g2[l],
                    rwkv_k_k[l], rwkv_k_a[l], rwkv_r_k[l], rwkv_ln_w[l], rwkv_ln_b[l], tile)
        y_d = stick_breaking(pd, min(SB_BLOCK, seq))
        x = out_projection(x, (y_a, y_b, y_c, y_d), gt_m, w_out[l].astype(BF16), tile)
        x = conv_ffn(x, sc_f, sh_f, gt_f, norm2_g[l], ffn_up[l], ffn_conv_w[l], ffn_conv_b[l],
                     ffn_down[l], tile, FFN_CHUNK)
    return x
```

```python
import functools

import jax
import jax.numpy as jnp
from jax import lax
from jax.experimental import pallas as pl
from jax.experimental.pallas import tpu as pltpu

F32 = jnp.float32
BF16 = jnp.bfloat16
HI = lax.Precision.HIGHEST

HEAD_DIM = 64
N_HEADS = 4
GROUP = HEAD_DIM * N_HEADS
CHUNK = 64
ATTN_LEFT_CHUNKS = 8
ATTN_MAX_REL = 128
LRU_CONV = 4
LRU_C = 8.0
RWKV_LN_EPS = 64e-5
FFN_CONV = 3
NORM_EPS = 1e-6
HALO = 8
NEG_BIG = -1e30
SB_EXIT = -104.0

VMEM_LIMIT = 56 * 1024 * 1024

ATTN_QB = 512
SEQ_TILE = 512
SB_BLOCK = 128
RWKV_CHUNK = 64
FFN_CHUNK = 256


def _params(*sem):
    return pltpu.CompilerParams(dimension_semantics=sem, vmem_limit_bytes=VMEM_LIMIT)


def _sigmoid(x):
    return 1.0 / (1.0 + jnp.exp(-x))


def _softplus(x):
    return jnp.maximum(x, 0.0) + jnp.log(1.0 + jnp.exp(-jnp.abs(x)))


def _gelu_tanh(x):
    return 0.5 * x * (1.0 + jnp.tanh(0.7978845608028654 * (x + 0.044715 * x * x * x)))


def _dot_nt(a, b, **kw):
    return lax.dot_general(a, b, (((1,), (1,)), ((), ())), preferred_element_type=F32, **kw)


def _dot_tn(a, b, **kw):
    return lax.dot_general(a, b, (((0,), (0,)), ((), ())), preferred_element_type=F32, **kw)


def _dot(a, b, **kw):
    return jnp.dot(a, b, preferred_element_type=F32, **kw)


def _mm(dot_fn, a, b, prec):
    if prec is None:
        return dot_fn(a.astype(BF16), b.astype(BF16))
    return dot_fn(a, b, precision=prec)


RWKV_PREC = {"gram": None, "inv": None, "apply": None, "state": None}


def _const_spec(shape):
    nd = len(shape)
    return pl.BlockSpec(shape, lambda *_: (0,) * nd)


def _ada_kernel(c_ref, w_ref, b_ref, o_ref):
    c = c_ref[...]
    s = c * _sigmoid(c)
    o_ref[0] = _dot(s, w_ref[0], precision=HI) + b_ref[0]


def ada_modulation(c, ada_w, ada_b):
    nl, d, n = ada_w.shape
    b = c.shape[0]
    rows = 8
    cp = jnp.zeros((rows, d), F32).at[:b].set(c)
    tn = 1536
    out = pl.pallas_call(
        _ada_kernel,
        grid=(nl, n // tn),
        in_specs=[
            pl.BlockSpec((rows, d), lambda l, j: (0, 0)),
            pl.BlockSpec((1, d, tn), lambda l, j: (l, 0, j)),
            pl.BlockSpec((1, 1, tn), lambda l, j: (l, 0, j)),
        ],
        out_specs=pl.BlockSpec((1, rows, tn), lambda l, j: (l, 0, j)),
        out_shape=jax.ShapeDtypeStruct((nl, rows, n), F32),
        compiler_params=_params("arbitrary", "arbitrary"),
        name="ada_modulation",
    )(cp, ada_w, ada_b.reshape(nl, 1, n))
    return out[:, :b]


def _modulated_norm(x, g, sc, sh):
    ms = jnp.mean(x * x, axis=-1, keepdims=True)
    return (x * lax.rsqrt(ms + NORM_EPS) * g) * (1.0 + sc) + sh


def _inproj_kernel(x_ref, sc_ref, sh_ref, g_ref, w_ref, oa_ref, ob_ref, oc_ref, od_ref, *, cuts):
    h = _modulated_norm(x_ref[0], g_ref[...], sc_ref[0], sh_ref[0]).astype(BF16)
    c0, c1, c2, c3 = cuts
    oa_ref[0] = _dot(h, w_ref[:, 0:c0])
    ob_ref[0] = _dot(h, w_ref[:, c0:c1])
    oc_ref[0] = _dot(h, w_ref[:, c1:c2])
    od_ref[0] = _dot(h, w_ref[:, c2:c3]).astype(BF16)


def in_projection(x, sc, sh, g, w_bf16, widths, tile):
    b, s, d = x.shape
    cuts = tuple(int(sum(widths[:i + 1])) for i in range(4))
    row = lambda bi, i: (bi, i, 0)
    vec = lambda bi, i: (bi, 0, 0)
    dts = (F32, F32, F32, BF16)
    return pl.pallas_call(
        functools.partial(_inproj_kernel, cuts=cuts),
        grid=(b, s // tile),
        in_specs=[
            pl.BlockSpec((1, tile, d), row),
            pl.BlockSpec((1, 1, d), vec),
            pl.BlockSpec((1, 1, d), vec),
            _const_spec((1, d)),
            _const_spec(w_bf16.shape),
        ],
        out_specs=[pl.BlockSpec((1, tile, w), row) for w in widths],
        out_shape=[jax.ShapeDtypeStruct((b, s, w), dt) for w, dt in zip(widths, dts)],
        compiler_params=_params("arbitrary", "arbitrary"),
        name="in_projection",
    )(x, sc, sh, g.reshape(1, d), w_bf16)


def _head_rms(x, gain):
    ms = jnp.mean(x * x, axis=-1, keepdims=True)
    return x * lax.rsqrt(ms + NORM_EPS) * gain


def _attn_kernel(q_ref, kp_ref, kc_ref, vp_ref, vc_ref, qg_ref, kg_ref, bias_ref, o_ref):
    i = pl.program_id(1)
    qb = q_ref.shape[1]
    col = lax.broadcasted_iota(jnp.int32, (qb, 2 * qb), 1)
    dead = jnp.logical_and(i == 0, col < qb)
    for h in range(N_HEADS):
        lanes = slice(h * HEAD_DIM, (h + 1) * HEAD_DIM)
        q = _head_rms(q_ref[0, :, lanes], qg_ref[...]).astype(BF16)
        k = jnp.concatenate([kp_ref[0, :, lanes], kc_ref[0, :, lanes]], axis=0)
        k = _head_rms(k, kg_ref[...]).astype(BF16)
        v = jnp.concatenate([vp_ref[0, :, lanes], vc_ref[0, :, lanes]], axis=0).astype(BF16)
        s = _dot_nt(q, k) * (HEAD_DIM ** -0.5) + bias_ref[h]
        s = jnp.where(dead, NEG_BIG, s)
        m = jnp.max(s, axis=-1, keepdims=True)
        p = jnp.exp(s - m)
        den = jnp.sum(p, axis=-1, keepdims=True)
        o_ref[0, :, lanes] = _dot(p.astype(BF16), v) / den


def attention_bias(rel_bias, qb):
    nh = rel_bias.shape[0]
    nk = 2 * qb
    qi = jnp.arange(qb)[:, None]
    kj = jnp.arange(nk)[None, :]
    qc = qi // CHUNK + qb // CHUNK
    kc = kj // CHUNK
    valid = jnp.logical_and(kc <= qc, kc >= qc - ATTN_LEFT_CHUNKS)
    m = jnp.arange(nk + 1)
    rel = jnp.where(m < qb + CHUNK,
                    jnp.clip(qb - m, -ATTN_MAX_REL, ATTN_MAX_REL) + ATTN_MAX_REL, 2 * ATTN_MAX_REL)
    vec = rel_bias.astype(F32)[:, rel]
    toeplitz = jnp.tile(vec, (1, qb))[:, :qb * nk].reshape(nh, qb, nk)
    return jnp.where(valid[None], toeplitz, NEG_BIG)


def chunk_attention(pa, q_gain, k_gain, bias, qb):
    b, s, _ = pa.shape
    blk = (1, qb, GROUP)
    prev = lambda bi, i, c: (bi, jnp.maximum(i - 1, 0), c)
    return pl.pallas_call(
        _attn_kernel,
        grid=(b, s // qb),
        in_specs=[
            pl.BlockSpec(blk, lambda bi, i: (bi, i, 0)),
            pl.BlockSpec(blk, lambda bi, i: prev(bi, i, 1)),
            pl.BlockSpec(blk, lambda bi, i: (bi, i, 1)),
            pl.BlockSpec(blk, lambda bi, i: prev(bi, i, 2)),
            pl.BlockSpec(blk, lambda bi, i: (bi, i, 2)),
            _const_spec((1, HEAD_DIM)),
            _const_spec((1, HEAD_DIM)),
            _const_spec(bias.shape),
        ],
        out_specs=pl.BlockSpec(blk, lambda bi, i: (bi, i, 0)),
        out_shape=jax.ShapeDtypeStruct((b, s, GROUP), F32),
        compiler_params=_params("arbitrary", "arbitrary"),
        name="chunk_attention",
    )(pa, pa, pa, pa, pa, q_gain.reshape(1, HEAD_DIM), k_gain.reshape(1, HEAD_DIM), bias)


def _lru_kernel(p_ref, cw_ref, cb_ref, raw_ref, rab_ref, riw_ref, rib_ref, lam_ref, o_ref,
                xpad_ref, h_ref):
    i = pl.program_id(1)
    t = p_ref.shape[1]

    @pl.when(i == 0)
    def _():
        xpad_ref[0:HALO, :] = jnp.zeros((HALO, GROUP), F32)
        h_ref[...] = jnp.zeros_like(h_ref)

    @pl.when(i > 0)
    def _():
        xpad_ref[0:HALO, :] = xpad_ref[t:t + HALO, :]

    xpad_ref[HALO:HALO + t, :] = p_ref[0, :, 0:GROUP]
    gate = p_ref[0, :, GROUP:2 * GROUP]
    xc = cb_ref[...]
    for j in range(LRU_CONV):
        xc = xc + cw_ref[j:j + 1, :] * xpad_ref[pl.ds(HALO - (LRU_CONV - 1) + j, t), :]
    xcb = xc.astype(BF16)
    r_gate = _sigmoid(_dot(xcb, raw_ref[...]) + rab_ref[...])
    i_gate = _sigmoid(_dot(xcb, riw_ref[...]) + rib_ref[...])
    log_a = (-LRU_C * r_gate) * _softplus(-lam_ref[...])
    a = jnp.exp(log_a)
    u = jnp.sqrt(-jnp.tanh(log_a) * (a * a + 1.0)) * (i_gate * xc)
    rows = lax.broadcasted_iota(jnp.int32, (t, 1), 0)
    k = 1
    while k < t:
        keep = rows >= k
        a_sh = jnp.where(keep, pltpu.roll(a, k, 0), 1.0)
        u_sh = jnp.where(keep, pltpu.roll(u, k, 0), 0.0)
        u = a * u_sh + u
        a = a * a_sh
        k *= 2
    hh = a * h_ref[...] + u
    h_ref[...] = hh[t - 1:t, :]
    o_ref[0] = hh * _gelu_tanh(gate)


def _block_diag(w):
    h, n, _ = w.shape
    eye = jnp.eye(h, dtype=w.dtype)
    return (eye[:, None, :, None] * w[:, :, None, :]).reshape(h * n, h * n)


def rglru(pb, conv_w, conv_b, ra_w, ra_b, ri_w, ri_b, lam, tile):
    b, s, _ = pb.shape
    row = lambda bi, i: (bi, i, 0)
    vec = lambda a: a.reshape(1, GROUP)
    return pl.pallas_call(
        _lru_kernel,
        grid=(b, s // tile),
        in_specs=[
            pl.BlockSpec((1, tile, 2 * GROUP), row),
            _const_spec((LRU_CONV, GROUP)),
            _const_spec((1, GROUP)),
            _const_spec((GROUP, GROUP)),
            _const_spec((1, GROUP)),
            _const_spec((GROUP, GROUP)),
            _const_spec((1, GROUP)),
            _const_spec((1, GROUP)),
        ],
        out_specs=pl.BlockSpec((1, tile, GROUP), row),
        out_shape=jax.ShapeDtypeStruct((b, s, GROUP), F32),
        scratch_shapes=[pltpu.VMEM((tile + HALO, GROUP), F32), pltpu.VMEM((1, GROUP), F32)],
        compiler_params=_params("arbitrary", "arbitrary"),
        name="rglru",
    )(pb, conv_w, vec(conv_b), _block_diag(ra_w).astype(BF16), vec(ra_b),
      _block_diag(ri_w).astype(BF16), vec(ri_b), vec(lam))


def _rwkv_kernel(p_ref, mu_ref, w0_ref, w2_ref, a0_ref, a2_ref, g2_ref, kk_ref, ka_ref, rk_ref,
                 lnw_ref, lnb_ref, o_ref,
                 ppad_ref, state_ref, logd_ref, r_ref, k_ref, v_ref, kk_s_ref, kka_ref, y_ref):
    i = pl.program_id(1)
    t = p_ref.shape[1]
    width = p_ref.shape[2]
    ch = RWKV_CHUNK
    g3 = 3 * GROUP

    @pl.when(i == 0)
    def _():
        ppad_ref[0:HALO, :] = jnp.zeros((HALO, width), F32)
        state_ref[...] = jnp.zeros_like(state_ref)

    @pl.when(i > 0)
    def _():
        ppad_ref[0:HALO, :] = ppad_ref[t:t + HALO, :]

    p = p_ref[0]
    ppad_ref[HALO:HALO + t, :] = p
    p_prev = ppad_ref[pl.ds(HALO - 1, t), :]
    p = p + (p_prev - p) * mu_ref[...]
    r = p[:, 0:GROUP]
    k = p[:, GROUP:2 * GROUP]
    v = p[:, 2 * GROUP:g3]
    low = p[:, g3:width]
    w = -_softplus(-(w0_ref[...] + _dot(jnp.tanh(low).astype(BF16), w2_ref[...]))) - 0.5
    a = _sigmoid(a0_ref[...] + _dot(low.astype(BF16), a2_ref[...]))
    g = _dot(_sigmoid(low).astype(BF16), g2_ref[...])

    hr = lax.broadcasted_iota(jnp.int32, (GROUP, GROUP), 0) // HEAD_DIM
    hc = lax.broadcasted_iota(jnp.int32, (GROUP, GROUP), 1) // HEAD_DIM
    head_ones = (hr == hc).astype(F32)

    def head_sum(x):
        return _dot(x, head_ones, precision=HI)

    kk = k * kk_ref[...]
    kk = kk / jnp.maximum(jnp.sqrt(head_sum(kk * kk)), 1e-12)
    k = k * (1.0 + (a - 1.0) * ka_ref[...])

    logd_ref[...] = -jnp.exp(w)
    r_ref[...] = r
    k_ref[...] = k
    v_ref[...] = v
    kk_s_ref[...] = kk
    kka_ref[...] = kk * a

    rr = lax.broadcasted_iota(jnp.int32, (ch, ch), 0)
    cc = lax.broadcasted_iota(jnp.int32, (ch, ch), 1)
    lower_incl = rr >= cc
    lower_strict = rr > cc
    tril = lower_incl.astype(F32)
    eye = (rr == cc).astype(F32)

    def chunk_body(c, carry):
        rows = pl.ds(pl.multiple_of(c * ch, ch), ch)
        logd = logd_ref[rows, :]
        cum = _dot(tril, logd, precision=HI)
        cum_end = cum[ch - 1:ch, :]
        p_in = jnp.exp(cum)
        p_inv = jnp.exp(-cum)
        p_end = jnp.exp(cum_end - cum)
        kk_c = kk_s_ref[rows, :]
        kka_c = kka_ref[rows, :]
        k_c = k_ref[rows, :]
        a_t = -kk_c * jnp.exp(cum - logd)
        b_t = kka_c * p_inv
        k_t = k_c * p_inv
        r_t = r_ref[rows, :] * p_in
        b_h = kka_c * p_end
        k_h = k_c * p_end
        decay_end = jnp.exp(cum_end)
        v_c = v_ref[rows, :]
        for h in range(N_HEADS):
            ln = slice(h * HEAD_DIM, (h + 1) * HEAD_DIM)
            ah, bh, kh, rh, vh = a_t[:, ln], b_t[:, ln], k_t[:, ln], r_t[:, ln], v_c[:, ln]
            n = jnp.where(lower_strict, _mm(_dot_nt, ah, bh, RWKV_PREC["gram"]), 0.0)
            a_ak = jnp.where(lower_strict, _mm(_dot_nt, ah, kh, RWKV_PREC["gram"]), 0.0)
            a_rb = jnp.where(lower_incl, _mm(_dot_nt, rh, bh, RWKV_PREC["gram"]), 0.0)
            a_rk = jnp.where(lower_incl, _mm(_dot_nt, rh, kh, RWKV_PREC["gram"]), 0.0)
            tinv = eye + n
            npow = n
            steps = 1
            while 2 * steps < ch:
                npow = _mm(_dot, npow, npow, RWKV_PREC["inv"])
                tinv = tinv + _mm(_dot, tinv, npow, RWKV_PREC["inv"])
                steps *= 2
            st = state_ref[h]
            pa = RWKV_PREC["apply"]
            rhs = _mm(_dot_nt, ah, st, pa) + _mm(_dot, a_ak, vh, pa)
            u = _mm(_dot, tinv, rhs, pa)
            y = _mm(_dot_nt, rh, st, pa) + _mm(_dot, a_rb, u, pa) + _mm(_dot, a_rk, vh, pa)
            state_ref[h] = (st * decay_end[:, ln] + _mm(_dot_tn, u, b_h[:, ln], RWKV_PREC["state"])
                            + _mm(_dot_tn, vh, k_h[:, ln], RWKV_PREC["state"]))
            y_ref[rows, ln] = y
        return carry

    lax.fori_loop(0, t // ch, chunk_body, 0)

    y = y_ref[...]
    mean = head_sum(y) * (1.0 / HEAD_DIM)
    yc = y - mean
    var = head_sum(yc * yc) * (1.0 / HEAD_DIM)
    y = yc * lax.rsqrt(var + RWKV_LN_EPS) * lnw_ref[...] + lnb_ref[...]
    r = r_ref[...]
    k = k_ref[...]
    v = v_ref[...]
    y = y + head_sum(r * k * rk_ref[...]) * v
    o_ref[0] = y * g


def _pad_rows(w, start, total):
    return jnp.zeros((total, w.shape[1]), w.dtype).at[start:start + w.shape[0]].set(w)


def rwkv7(pc, mu, w0, w2, a0, a2, g2, k_k, k_a, r_k, ln_w, ln_b, tile):
    b, s, width = pc.shape
    low = width - 3 * GROUP
    dr, ir = w2.shape[0], a2.shape[0]
    row = lambda bi, i: (bi, i, 0)
    vec = lambda a: a.reshape(1, -1)
    tile_buf = pltpu.VMEM((tile, GROUP), F32)
    return pl.pallas_call(
        _rwkv_kernel,
        grid=(b, s // tile),
        in_specs=[
            pl.BlockSpec((1, tile, width), row),
            _const_spec((1, width)),
            _const_spec((1, GROUP)),
            _const_spec((low, GROUP)),
            _const_spec((1, GROUP)),
            _const_spec((low, GROUP)),
            _const_spec((low, GROUP)),
            _const_spec((1, GROUP)),
            _const_spec((1, GROUP)),
            _const_spec((1, GROUP)),
            _const_spec((1, GROUP)),
            _const_spec((1, GROUP)),
        ],
        out_specs=pl.BlockSpec((1, tile, GROUP), row),
        out_shape=jax.ShapeDtypeStruct((b, s, GROUP), F32),
        scratch_shapes=[
            pltpu.VMEM((tile + HALO, width), F32),
            pltpu.VMEM((N_HEADS, HEAD_DIM, HEAD_DIM), F32),
            tile_buf, tile_buf, tile_buf, tile_buf, tile_buf, tile_buf, tile_buf,
        ],
        compiler_params=_params("arbitrary", "arbitrary"),
        name="rwkv7",
    )(pc, vec(mu), vec(w0), _pad_rows(w2, 0, low).astype(BF16), vec(a0),
      _pad_rows(a2, dr, low).astype(BF16), _pad_rows(g2, dr + ir, low).astype(BF16),
      vec(k_k), vec(k_a), vec(r_k), vec(ln_w), vec(ln_b))


def _sb_kernel(q_ref, k_ref, v_ref, o_ref):
    qi = pl.program_id(1)
    blk = q_ref.shape[1]
    rr = lax.broadcasted_iota(jnp.int32, (blk, 2 * blk), 0)
    cc = lax.broadcasted_iota(jnp.int32, (blk, 2 * blk), 1)
    sums = jnp.logical_or(rr >= cc, cc >= blk).astype(BF16)
    causal = (lax.broadcasted_iota(jnp.int32, (blk, blk), 1) < lax.broadcasted_iota(jnp.int32, (blk, blk), 0))
    scale = HEAD_DIM ** -0.5
    qs = [q_ref[0, :, h * HEAD_DIM:(h + 1) * HEAD_DIM] * scale for h in range(N_HEADS)]

    heads = range(N_HEADS)
    lanes = [slice(h * HEAD_DIM, (h + 1) * HEAD_DIM) for h in heads]

    def tiles(kb, diagonal, runs):
        rows = pl.ds(pl.multiple_of(kb * blk, blk), blk)
        zs = [_dot_nt(qs[h], k_ref[0, rows, lanes[h]]) for h in heads]
        lks = [-_softplus(z) for z in zs]
        if diagonal:
            lks = [jnp.where(causal, lk, 0.0) for lk in lks]
        his = [lk.astype(BF16) for lk in lks]
        los = [(lk - hi.astype(F32)).astype(BF16) for lk, hi in zip(lks, his)]
        ss = [_dot(hi, sums) + _dot(lo, sums) for hi, lo in zip(his, los)]
        if diagonal:
            logw = [jnp.where(causal, z + s[:, :blk], NEG_BIG) for z, s in zip(zs, ss)]
            totals = [s[:, blk:] for s in ss]
        else:
            logw = [z + s[:, :blk] + run for z, s, run in zip(zs, ss, runs)]
            totals = [run + s[:, blk:] for s, run in zip(ss, runs)]
        pvs = [_dot(jnp.exp(lw).astype(BF16), v_ref[0, rows, lanes[h]]) for h, lw in zip(heads, logw)]
        return pvs, totals

    accs, runs = tiles(qi, True, None)

    def alive(runs):
        top = functools.reduce(jnp.maximum, runs)[:, 0:1]
        return jnp.max(top, axis=0, keepdims=True)[0, 0] > SB_EXIT

    def cond(carry):
        j, live, _, _ = carry
        return jnp.logical_and(j < qi, live)

    def body(carry):
        j, _, accs, runs = carry
        pvs, runs = tiles(qi - 1 - j, False, runs)
        return j + 1, alive(runs), tuple(a + p for a, p in zip(accs, pvs)), tuple(runs)

    _, _, accs, _ = lax.while_loop(cond, body, (0, alive(runs), tuple(accs), tuple(runs)))
    for h in heads:
        o_ref[0, :, lanes[h]] = accs[h]


def stick_breaking(pd_bf16, blk):
    b, s, _ = pd_bf16.shape
    return pl.pallas_call(
        _sb_kernel,
        grid=(b, s // blk),
        in_specs=[
            pl.BlockSpec((1, blk, GROUP), lambda bi, i: (bi, i, 0)),
            pl.BlockSpec((1, s, GROUP), lambda bi, i: (bi, 0, 1)),
            pl.BlockSpec((1, s, GROUP), lambda bi, i: (bi, 0, 2)),
        ],
        out_specs=pl.BlockSpec((1, blk, GROUP), lambda bi, i: (bi, i, 0)),
        out_shape=jax.ShapeDtypeStruct((b, s, GROUP), F32),
        compiler_params=_params("arbitrary", "arbitrary"),
        name="stick_breaking",
    )(pd_bf16, pd_bf16, pd_bf16)


def _outproj_kernel(x_ref, ya_ref, yb_ref, yc_ref, yd_ref, gt_ref, w_ref, o_ref):
    y = jnp.concatenate([ya_ref[0], yb_ref[0], yc_ref[0], yd_ref[0]], axis=-1).astype(BF16)
    o_ref[0] = x_ref[0] + gt_ref[0] * _dot(y, w_ref[...])


def out_projection(x, ys, gt, w_bf16, tile):
    b, s, d = x.shape
    row = lambda bi, i: (bi, i, 0)
    return pl.pallas_call(
        _outproj_kernel,
        grid=(b, s // tile),
        in_specs=[pl.BlockSpec((1, tile, d), row)]
        + [pl.BlockSpec((1, tile, GROUP), row) for _ in ys]
        + [pl.BlockSpec((1, 1, d), lambda bi, i: (bi, 0, 0)), _const_spec(w_bf16.shape)],
        out_specs=pl.BlockSpec((1, tile, d), row),
        out_shape=jax.ShapeDtypeStruct((b, s, d), F32),
        compiler_params=_params("arbitrary", "arbitrary"),
        name="out_projection",
    )(x, *ys, gt, w_bf16)


def _ffn_kernel(x_ref, xh_ref, sc_ref, sh_ref, gt_ref, g_ref, wv_ref, wg_ref, cwv_ref, cwg_ref,
                cbv_ref, cbg_ref, wd_ref, o_ref, uv_ref, ug_ref, acc_ref):
    i = pl.program_id(1)
    t = x_ref.shape[1]
    x = x_ref[0]
    xe = jnp.concatenate([xh_ref[0], x], axis=0)
    h = _modulated_norm(xe, g_ref[...], sc_ref[0], sh_ref[0]).astype(BF16)
    live = jnp.logical_or(i > 0, lax.broadcasted_iota(jnp.int32, (t + HALO, 1), 0) >= HALO)
    acc_ref[...] = jnp.zeros_like(acc_ref)

    def conv(u_ref, cw, cb):
        out = cb
        for j in range(FFN_CONV):
            out = out + cw[j:j + 1, :] * u_ref[pl.ds(HALO - (FFN_CONV - 1) + j, t), :]
        return out

    def body(f, carry):
        uv_ref[...] = jnp.where(live, _dot(h, wv_ref[f]), 0.0)
        ug_ref[...] = jnp.where(live, _dot(h, wg_ref[f]), 0.0)
        val = conv(uv_ref, cwv_ref[f], cbv_ref[f])
        gate = conv(ug_ref, cwg_ref[f], cbg_ref[f])
        acc_ref[...] += _dot((val * _gelu_tanh(gate)).astype(BF16), wd_ref[f])
        return carry

    lax.fori_loop(0, wv_ref.shape[0], body, 0)
    o_ref[0] = x + gt_ref[0] * acc_ref[...]


def conv_ffn(x, sc, sh, gt, g, w_up, conv_w, conv_b, w_down, tile, fchunk):
    b, s, d = x.shape
    f = w_down.shape[0]
    nf = f // fchunk
    row = lambda bi, i: (bi, i, 0)
    vec = lambda bi, i: (bi, 0, 0)
    halo = lambda bi, i: (bi, jnp.maximum(i * (tile // HALO) - 1, 0), 0)
    cols = lambda w: w.reshape(w.shape[0], nf, fchunk).transpose(1, 0, 2)
    wv = cols(w_up[:, :f]).astype(BF16)
    wg = cols(w_up[:, f:]).astype(BF16)
    wd = w_down.reshape(nf, fchunk, d).astype(BF16)
    cwv, cwg = cols(conv_w[:, :f]), cols(conv_w[:, f:])
    cbv, cbg = cols(conv_b[None, :f]), cols(conv_b[None, f:])
    return pl.pallas_call(
        _ffn_kernel,
        grid=(b, s // tile),
        in_specs=[
            pl.BlockSpec((1, tile, d), row),
            pl.BlockSpec((1, HALO, d), halo),
            pl.BlockSpec((1, 1, d), vec),
            pl.BlockSpec((1, 1, d), vec),
            pl.BlockSpec((1, 1, d), vec),
            _const_spec((1, d)),
            _const_spec(wv.shape),
            _const_spec(wg.shape),
            _const_spec(cwv.shape),
            _const_spec(cwg.shape),
            _const_spec(cbv.shape),
            _const_spec(cbg.shape),
            _const_spec(wd.shape),
        ],
        out_specs=pl.BlockSpec((1, tile, d), row),
        out_shape=jax.ShapeDtypeStruct((b, s, d), F32),
        scratch_shapes=[
            pltpu.VMEM((tile + HALO, fchunk), F32),
            pltpu.VMEM((tile + HALO, fchunk), F32),
            pltpu.VMEM((tile, d), F32),
        ],
        compiler_params=_params("arbitrary", "arbitrary"),
        name="conv_ffn",
    )(x, x, sc, sh, gt, g.reshape(1, d), wv, wg, cwv, cwg, cbv, cbg, wd)


def kernel(x, c, ada_w, ada_b, norm1_g, norm2_g, w_in, w_out, attn_q_gain, attn_k_gain, attn_rel_bias, lru_conv_w, lru_conv_b, lru_ra_w, lru_ra_b, lru_ri_w, lru_ri_b, lru_lambda, rwkv_mu, rwkv_w0, rwkv_w2, rwkv_a0, rwkv_a2, rwkv_g2, rwkv_k_k, rwkv_k_a, rwkv_r_k, rwkv_ln_w, rwkv_ln_b, ffn_up, ffn_conv_w, ffn_conv_b, ffn_down):
    depth = ada_w.shape[0]
    bsz, seq, d = x.shape
    p_rwkv = rwkv_mu.shape[1]
    widths = (3 * GROUP, 2 * GROUP, p_rwkv, 3 * GROUP)
    tile = min(SEQ_TILE, seq)
    qb = min(ATTN_QB, seq)
    mods = ada_modulation(c, ada_w, ada_b)
    for l in range(depth):
        sh_m, sc_m, gt_m, sh_f, sc_f, gt_f = [
            mods[l, :, j * d:(j + 1) * d].reshape(bsz, 1, d) for j in range(6)]
        pa, pb, pc, pd = in_projection(x, sc_m, sh_m, norm1_g[l], w_in[l].astype(BF16), widths, tile)
        y_a = chunk_attention(pa, attn_q_gain[l], attn_k_gain[l],
                              attention_bias(attn_rel_bias[l], qb), qb)
        y_b = rglru(pb, lru_conv_w[l], lru_conv_b[l], lru_ra_w[l], lru_ra_b[l],
                    lru_ri_w[l], lru_ri_b[l], lru_lambda[l], tile)
        y_c = rwkv7(pc, rwkv_mu[l], rwkv_w0[l], rwkv_w2[l], rwkv_a0[l], rwkv_a2[l], rwkv_g2[l],
                    rwkv_k_k[l], rwkv_k_a[l], rwkv_r_k[l], rwkv_ln_w[l], rwkv_ln_b[l], tile)
        y_d = stick_breaking(pd, min(SB_BLOCK, seq))
        x = out_projection(x, (y_a, y_b, y_c, y_d), gt_m, w_out[l].astype(BF16), tile)
        x = conv_ffn(x, sc_f, sh_f, gt_f, norm2_g[l], ffn_up[l], ffn_conv_w[l], ffn_conv_b[l],
                     ffn_down[l], tile, FFN_CHUNK)
    return x
```

```python
import functools

import jax
import jax.numpy as jnp
from jax import lax
from jax.experimental import pallas as pl
from jax.experimental.pallas import tpu as pltpu

F32 = jnp.float32
BF16 = jnp.bfloat16
HI = lax.Precision.HIGHEST

HEAD_DIM = 64
N_HEADS = 4
GROUP = HEAD_DIM * N_HEADS
CHUNK = 64
ATTN_LEFT_CHUNKS = 8
ATTN_MAX_REL = 128
LRU_CONV = 4
LRU_C = 8.0
RWKV_LN_EPS = 64e-5
FFN_CONV = 3
NORM_EPS = 1e-6
HALO = 8
NEG_BIG = -1e30
SB_EXIT = -104.0

VMEM_LIMIT = 56 * 1024 * 1024

ATTN_QB = 256
SEQ_TILE = 512
SB_BLOCK = 128
RWKV_CHUNK = 64
FFN_CHUNK = 256


def _params(*sem):
    return pltpu.CompilerParams(dimension_semantics=sem, vmem_limit_bytes=VMEM_LIMIT)


def _sigmoid(x):
    return 1.0 / (1.0 + jnp.exp(-x))


def _softplus(x):
    return jnp.maximum(x, 0.0) + jnp.log(1.0 + jnp.exp(-jnp.abs(x)))


def _gelu_tanh(x):
    return 0.5 * x * (1.0 + jnp.tanh(0.7978845608028654 * (x + 0.044715 * x * x * x)))


def _dot_nt(a, b, **kw):
    return lax.dot_general(a, b, (((1,), (1,)), ((), ())), preferred_element_type=F32, **kw)


def _dot_tn(a, b, **kw):
    return lax.dot_general(a, b, (((0,), (0,)), ((), ())), preferred_element_type=F32, **kw)


def _dot(a, b, **kw):
    return jnp.dot(a, b, preferred_element_type=F32, **kw)


def _const_spec(shape):
    nd = len(shape)
    return pl.BlockSpec(shape, lambda *_: (0,) * nd)


def _ada_kernel(c_ref, w_ref, b_ref, o_ref):
    c = c_ref[...]
    s = c * _sigmoid(c)
    o_ref[0] = _dot(s, w_ref[0], precision=HI) + b_ref[0]


def ada_modulation(c, ada_w, ada_b):
    nl, d, n = ada_w.shape
    b = c.shape[0]
    rows = 8
    cp = jnp.zeros((rows, d), F32).at[:b].set(c)
    tn = 1536
    out = pl.pallas_call(
        _ada_kernel,
        grid=(nl, n // tn),
        in_specs=[
            pl.BlockSpec((rows, d), lambda l, j: (0, 0)),
            pl.BlockSpec((1, d, tn), lambda l, j: (l, 0, j)),
            pl.BlockSpec((1, 1, tn), lambda l, j: (l, 0, j)),
        ],
        out_specs=pl.BlockSpec((1, rows, tn), lambda l, j: (l, 0, j)),
        out_shape=jax.ShapeDtypeStruct((nl, rows, n), F32),
        compiler_params=_params("arbitrary", "arbitrary"),
        name="ada_modulation",
    )(cp, ada_w, ada_b.reshape(nl, 1, n))
    return out[:, :b]


def _modulated_norm(x, g, sc, sh):
    ms = jnp.mean(x * x, axis=-1, keepdims=True)
    return (x * lax.rsqrt(ms + NORM_EPS) * g) * (1.0 + sc) + sh


def _inproj_kernel(x_ref, sc_ref, sh_ref, g_ref, w_ref, qkg_ref, oa_ref, ob_ref, oc_ref, od_ref, *, cuts):
    h = _modulated_norm(x_ref[0], g_ref[...], sc_ref[0], sh_ref[0]).astype(BF16)
    c0, c1, c2, c3 = cuts
    pa = _dot(h, w_ref[:, 0:c0])
    nqk = 2 * GROUP
    qk = pa[:, :nqk]
    sq = qk * qk
    hi = sq.astype(BF16)
    lo = (sq - hi.astype(F32)).astype(BF16)
    hr = lax.broadcasted_iota(jnp.int32, (nqk, nqk), 0) // HEAD_DIM
    hc = lax.broadcasted_iota(jnp.int32, (nqk, nqk), 1) // HEAD_DIM
    same_head = (hr == hc).astype(BF16)
    ms = (_dot(hi, same_head) + _dot(lo, same_head)) * (1.0 / HEAD_DIM)
    oa_ref[0, :, :nqk] = (qk * lax.rsqrt(ms + NORM_EPS) * qkg_ref[...]).astype(BF16)
    oa_ref[0, :, nqk:] = pa[:, nqk:].astype(BF16)
    ob_ref[0] = _dot(h, w_ref[:, c0:c1])
    oc_ref[0] = _dot(h, w_ref[:, c1:c2])
    od_ref[0] = _dot(h, w_ref[:, c2:c3]).astype(BF16)


def in_projection(x, sc, sh, g, w_bf16, q_gain, k_gain, widths, tile):
    b, s, d = x.shape
    cuts = tuple(int(sum(widths[:i + 1])) for i in range(4))
    row = lambda bi, i: (bi, i, 0)
    vec = lambda bi, i: (bi, 0, 0)
    dts = (BF16, F32, F32, BF16)
    qkg = jnp.concatenate([jnp.tile(q_gain.astype(F32), N_HEADS) * (HEAD_DIM ** -0.5),
                           jnp.tile(k_gain.astype(F32), N_HEADS)]).reshape(1, 2 * GROUP)
    return pl.pallas_call(
        functools.partial(_inproj_kernel, cuts=cuts),
        grid=(b, s // tile),
        in_specs=[
            pl.BlockSpec((1, tile, d), row),
            pl.BlockSpec((1, 1, d), vec),
            pl.BlockSpec((1, 1, d), vec),
            _const_spec((1, d)),
            _const_spec(w_bf16.shape),
            _const_spec((1, 2 * GROUP)),
        ],
        out_specs=[pl.BlockSpec((1, tile, w), row) for w in widths],
        out_shape=[jax.ShapeDtypeStruct((b, s, w), dt) for w, dt in zip(widths, dts)],
        compiler_params=_params("arbitrary", "arbitrary"),
        name="in_projection",
    )(x, sc, sh, g.reshape(1, d), w_bf16, qkg)


def _attn_kernel(*refs, nb):
    q_ref, k_refs, v_refs = refs[0], refs[1:1 + nb], refs[1 + nb:1 + 2 * nb]
    bias_ref, o_ref = refs[1 + 2 * nb:]
    heads = range(N_HEADS)
    lane = lax.broadcasted_iota(jnp.int32, (1, 2 * HEAD_DIM), 1)
    keep = [lane < HEAD_DIM, lane >= HEAD_DIM]
    groups = [slice(g * 2 * HEAD_DIM, (g + 1) * 2 * HEAD_DIM) for g in range(N_HEADS // 2)]
    q2 = [q_ref[0, :, g] for g in groups]
    k2 = [jnp.concatenate([r[0, :, g] for r in k_refs], axis=0) for g in groups]
    v2 = [jnp.concatenate([r[0, :, g] for r in v_refs], axis=0) for g in groups]
    qs = [jnp.where(keep[h % 2], q2[h // 2], jnp.zeros_like(q2[0])) for h in heads]
    ss = [_dot_nt(qs[h], k2[h // 2]) + bias_ref[0, h] for h in heads]
    ps = [jnp.exp(s - jnp.max(s, axis=-1, keepdims=True)) for s in ss]
    dens = [jnp.sum(p, axis=-1, keepdims=True) for p in ps]
    outs = [_dot(ps[h].astype(BF16), v2[h // 2]) / dens[h] for h in heads]
    for g, lanes in enumerate(groups):
        o_ref[0, :, lanes] = jnp.where(keep[0], outs[2 * g], outs[2 * g + 1])


def attention_bias(rel_bias, qb):
    nh = rel_bias.shape[0]
    nb = 1 + ATTN_LEFT_CHUNKS * CHUNK // qb
    nk = nb * qb
    back = nk - qb
    qi = jnp.arange(qb)[:, None]
    kj = jnp.arange(nk)[None, :]
    qc = (qi + back) // CHUNK
    kc = kj // CHUNK
    valid = jnp.logical_and(kc <= qc, kc >= qc - ATTN_LEFT_CHUNKS)
    m = jnp.arange(nk + 1)
    rel = jnp.where(m < back + CHUNK,
                    jnp.clip(back - m, -ATTN_MAX_REL, ATTN_MAX_REL) + ATTN_MAX_REL, 2 * ATTN_MAX_REL)
    vec = rel_bias.astype(F32)[:, rel]
    toeplitz = jnp.tile(vec, (1, qb))[:, :qb * nk].reshape(nh, qb, nk)
    first_key = (nb - 1 - jnp.arange(nb))[:, None, None, None] * qb
    live = jnp.logical_and(valid[None, None], kj[None, None] >= first_key)
    return jnp.where(live, toeplitz[None], NEG_BIG)


def chunk_attention(pa, bias, qb):
    b, s, _ = pa.shape
    nb = bias.shape[0]
    blk = (1, qb, GROUP)

    def window(col):
        return [pl.BlockSpec(blk, functools.partial(
            lambda bi, i, back: (bi, jnp.maximum(i - back, 0), col), back=back))
            for back in range(nb - 1, -1, -1)]

    return pl.pallas_call(
        functools.partial(_attn_kernel, nb=nb),
        grid=(b, s // qb),
        in_specs=[pl.BlockSpec(blk, lambda bi, i: (bi, i, 0))] + window(1) + window(2) + [
            pl.BlockSpec((1,) + bias.shape[1:], lambda bi, i: (jnp.minimum(i, nb - 1), 0, 0, 0)),
        ],
        out_specs=pl.BlockSpec(blk, lambda bi, i: (bi, i, 0)),
        out_shape=jax.ShapeDtypeStruct((b, s, GROUP), F32),
        compiler_params=_params("arbitrary", "arbitrary"),
        name="chunk_attention",
    )(*([pa] * (1 + 2 * nb)), bias)


def _lru_kernel(p_ref, cw_ref, cb_ref, raw_ref, rab_ref, riw_ref, rib_ref, lam_ref, o_ref,
                xpad_ref, h_ref):
    i = pl.program_id(1)
    t = p_ref.shape[1]

    @pl.when(i == 0)
    def _():
        xpad_ref[0:HALO, :] = jnp.zeros((HALO, GROUP), F32)
        h_ref[...] = jnp.zeros_like(h_ref)

    @pl.when(i > 0)
    def _():
        xpad_ref[0:HALO, :] = xpad_ref[t:t + HALO, :]

    xpad_ref[HALO:HALO + t, :] = p_ref[0, :, 0:GROUP]
    gate = p_ref[0, :, GROUP:2 * GROUP]
    xc = cb_ref[...]
    for j in range(LRU_CONV):
        xc = xc + cw_ref[j:j + 1, :] * xpad_ref[pl.ds(HALO - (LRU_CONV - 1) + j, t), :]
    xcb = xc.astype(BF16)
    r_gate = _sigmoid(_dot(xcb, raw_ref[...]) + rab_ref[...])
    i_gate = _sigmoid(_dot(xcb, riw_ref[...]) + rib_ref[...])
    log_a = (-LRU_C * r_gate) * _softplus(-lam_ref[...])
    a = jnp.exp(log_a)
    u = jnp.sqrt(-jnp.tanh(log_a) * (a * a + 1.0)) * (i_gate * xc)
    rows = lax.broadcasted_iota(jnp.int32, (t, 1), 0)
    k = 1
    while k < t:
        keep = rows >= k
        a_sh = jnp.where(keep, pltpu.roll(a, k, 0), 1.0)
        u_sh = jnp.where(keep, pltpu.roll(u, k, 0), 0.0)
        u = a * u_sh + u
        a = a * a_sh
        k *= 2
    hh = a * h_ref[...] + u
    h_ref[...] = hh[t - 1:t, :]
    o_ref[0] = hh * _gelu_tanh(gate)


def _block_diag(w):
    h, n, _ = w.shape
    eye = jnp.eye(h, dtype=w.dtype)
    return (eye[:, None, :, None] * w[:, :, None, :]).reshape(h * n, h * n)


def rglru(pb, conv_w, conv_b, ra_w, ra_b, ri_w, ri_b, lam, tile):
    b, s, _ = pb.shape
    row = lambda bi, i: (bi, i, 0)
    vec = lambda a: a.reshape(1, GROUP)
    return pl.pallas_call(
        _lru_kernel,
        grid=(b, s // tile),
        in_specs=[
            pl.BlockSpec((1, tile, 2 * GROUP), row),
            _const_spec((LRU_CONV, GROUP)),
            _const_spec((1, GROUP)),
            _const_spec((GROUP, GROUP)),
            _const_spec((1, GROUP)),
            _const_spec((GROUP, GROUP)),
            _const_spec((1, GROUP)),
            _const_spec((1, GROUP)),
        ],
        out_specs=pl.BlockSpec((1, tile, GROUP), row),
        out_shape=jax.ShapeDtypeStruct((b, s, GROUP), F32),
        scratch_shapes=[pltpu.VMEM((tile + HALO, GROUP), F32), pltpu.VMEM((1, GROUP), F32)],
        compiler_params=_params("arbitrary", "arbitrary"),
        name="rglru",
    )(pb, conv_w, vec(conv_b), _block_diag(ra_w).astype(BF16), vec(ra_b),
      _block_diag(ri_w).astype(BF16), vec(ri_b), vec(lam))


def _rwkv_kernel(p_ref, mu_ref, w0_ref, w2_ref, a0_ref, a2_ref, g2_ref, kk_ref, ka_ref, rk_ref,
                 lnw_ref, lnb_ref, o_ref, ppad_ref, state_ref, y_ref):
    i = pl.program_id(1)
    t = p_ref.shape[1]
    width = p_ref.shape[2]
    ch = RWKV_CHUNK
    g3 = 3 * GROUP

    @pl.when(i == 0)
    def _():
        ppad_ref[0:HALO, :] = jnp.zeros((HALO, width), F32)
        state_ref[...] = jnp.zeros_like(state_ref)

    @pl.when(i > 0)
    def _():
        ppad_ref[0:HALO, :] = ppad_ref[t:t + HALO, :]

    p = p_ref[0]
    ppad_ref[HALO:HALO + t, :] = p
    p_prev = ppad_ref[pl.ds(HALO - 1, t), :]
    p = p + (p_prev - p) * mu_ref[...]
    r = p[:, 0:GROUP]
    k = p[:, GROUP:2 * GROUP]
    v = p[:, 2 * GROUP:g3]
    low = p[:, g3:width]
    w = -_softplus(-(w0_ref[...] + _dot(jnp.tanh(low).astype(BF16), w2_ref[...]))) - 0.5
    a = _sigmoid(a0_ref[...] + _dot(low.astype(BF16), a2_ref[...]))
    g = _dot(_sigmoid(low).astype(BF16), g2_ref[...])

    hr = lax.broadcasted_iota(jnp.int32, (GROUP, GROUP), 0) // HEAD_DIM
    hc = lax.broadcasted_iota(jnp.int32, (GROUP, GROUP), 1) // HEAD_DIM
    head_ones = (hr == hc).astype(F32)

    def head_sum(x):
        return _dot(x, head_ones, precision=HI)

    kk = k * kk_ref[...]
    kk = kk / jnp.maximum(jnp.sqrt(head_sum(kk * kk)), 1e-12)
    k = k * (1.0 + (a - 1.0) * ka_ref[...])

    logd = -jnp.exp(w)
    kka = kk * a
    nch = t // ch
    heads = range(N_HEADS)
    lanes = [slice(h * HEAD_DIM, (h + 1) * HEAD_DIM) for h in heads]
    rows = [slice(c * ch, (c + 1) * ch) for c in range(nch)]
    pairs = [(c, h) for c in range(nch) for h in heads]

    def blk(x, c, h):
        return x[rows[c], lanes[h]]

    rr = lax.broadcasted_iota(jnp.int32, (ch, ch), 0)
    cc = lax.broadcasted_iota(jnp.int32, (ch, ch), 1)
    tril = (rr >= cc).astype(F32)
    eye = (rr == cc).astype(F32)
    gr = lax.broadcasted_iota(jnp.int32, (2 * ch, 2 * ch), 0)
    gc = lax.broadcasted_iota(jnp.int32, (2 * ch, 2 * ch), 1)
    gc = jnp.where(gc >= ch, gc - ch, gc)
    gram_mask = jnp.logical_or(jnp.logical_and(gr < ch, gc < gr),
                               jnp.logical_and(gr >= ch, gc <= gr - ch))

    cums = [_dot(tril, logd[rs, :], precision=HI) for rs in rows]
    cum = jnp.concatenate(cums, axis=0)
    cum_end = jnp.concatenate([jnp.broadcast_to(cm[ch - 1:ch, :], (ch, GROUP)) for cm in cums], axis=0)
    p_inv = jnp.exp(-cum)
    p_end = jnp.exp(cum_end - cum)
    a_t = (-kk * jnp.exp(cum - logd)).astype(BF16)
    b_t = (kka * p_inv).astype(BF16)
    k_t = (k * p_inv).astype(BF16)
    r_t = (r * jnp.exp(cum)).astype(BF16)
    b_h = (kka * p_end).astype(BF16)
    k_h = (k * p_end).astype(BF16)
    v_b = v.astype(BF16)
    decay_end = [jnp.exp(cm[ch - 1:ch, :]) for cm in cums]
    zeros = jnp.zeros((ch, HEAD_DIM), BF16)

    grams = [jnp.where(gram_mask,
                       _dot_nt(jnp.concatenate([blk(a_t, c, h), blk(r_t, c, h)], axis=0),
                               jnp.concatenate([blk(b_t, c, h), blk(k_t, c, h)], axis=0)), 0.0)
             for c, h in pairs]
    g_top = [g[:ch].astype(BF16) for g in grams]
    g_bot = [g[ch:].astype(BF16) for g in grams]
    tinvs = [eye + g[:ch, :ch] for g in grams]
    npows = [g[:ch, :ch].astype(BF16) for g in grams]
    steps = 1
    while 2 * steps < ch:
        npows = [_dot(pw, pw).astype(BF16) for pw in npows]
        tinvs = [ti + _dot(ti.astype(BF16), pw) for ti, pw in zip(tinvs, npows)]
        steps *= 2
    tinvs = [ti.astype(BF16) for ti in tinvs]
    ws = [_dot(gt, jnp.concatenate([zeros, blk(v_b, c, h)], axis=0)) for gt, (c, h) in zip(g_top, pairs)]
    gvs = [_dot_tn(blk(v_b, c, h), blk(k_h, c, h)) for c, h in pairs]

    sts = [state_ref[h] for h in heads]
    for c in range(nch):
        i0 = c * N_HEADS
        stb = [st.astype(BF16) for st in sts]
        rhs = [_dot_nt(blk(a_t, c, h), stb[h]) + ws[i0 + h] for h in heads]
        us = [_dot(tinvs[i0 + h], rhs[h].astype(BF16)).astype(BF16) for h in heads]
        ys = [_dot_nt(blk(r_t, c, h), stb[h])
              + _dot(g_bot[i0 + h], jnp.concatenate([us[h], blk(v_b, c, h)], axis=0)) for h in heads]
        sts = [sts[h] * decay_end[c][:, lanes[h]] + _dot_tn(us[h], blk(b_h, c, h)) + gvs[i0 + h]
               for h in heads]
        for h in heads:
            y_ref[rows[c], lanes[h]] = ys[h]
    for h in heads:
        state_ref[h] = sts[h]

    y = y_ref[...]
    mean = head_sum(y) * (1.0 / HEAD_DIM)
    yc = y - mean
    var = head_sum(yc * yc) * (1.0 / HEAD_DIM)
    y = yc * lax.rsqrt(var + RWKV_LN_EPS) * lnw_ref[...] + lnb_ref[...]
    y = y + head_sum(r * k * rk_ref[...]) * v
    o_ref[0] = y * g


def _pad_rows(w, start, total):
    return jnp.zeros((total, w.shape[1]), w.dtype).at[start:start + w.shape[0]].set(w)


def rwkv7(pc, mu, w0, w2, a0, a2, g2, k_k, k_a, r_k, ln_w, ln_b, tile):
    b, s, width = pc.shape
    low = width - 3 * GROUP
    dr, ir = w2.shape[0], a2.shape[0]
    row = lambda bi, i: (bi, i, 0)
    vec = lambda a: a.reshape(1, -1)
    return pl.pallas_call(
        _rwkv_kernel,
        grid=(b, s // tile),
        in_specs=[
            pl.BlockSpec((1, tile, width), row),
            _const_spec((1, width)),
            _const_spec((1, GROUP)),
            _const_spec((low, GROUP)),
            _const_spec((1, GROUP)),
            _const_spec((low, GROUP)),
            _const_spec((low, GROUP)),
            _const_spec((1, GROUP)),
            _const_spec((1, GROUP)),
            _const_spec((1, GROUP)),
            _const_spec((1, GROUP)),
            _const_spec((1, GROUP)),
        ],
        out_specs=pl.BlockSpec((1, tile, GROUP), row),
        out_shape=jax.ShapeDtypeStruct((b, s, GROUP), F32),
        scratch_shapes=[
            pltpu.VMEM((tile + HALO, width), F32),
            pltpu.VMEM((N_HEADS, HEAD_DIM, HEAD_DIM), F32),
            pltpu.VMEM((tile, GROUP), F32),
        ],
        compiler_params=_params("arbitrary", "arbitrary"),
        name="rwkv7",
    )(pc, vec(mu), vec(w0), _pad_rows(w2, 0, low).astype(BF16), vec(a0),
      _pad_rows(a2, dr, low).astype(BF16), _pad_rows(g2, dr + ir, low).astype(BF16),
      vec(k_k), vec(k_a), vec(r_k), vec(ln_w), vec(ln_b))


def _sb_kernel(q_ref, k_ref, v_ref, o_ref):
    qi = pl.program_id(1)
    blk = q_ref.shape[1]
    rr = lax.broadcasted_iota(jnp.int32, (blk, 2 * blk), 0)
    cc = lax.broadcasted_iota(jnp.int32, (blk, 2 * blk), 1)
    sums = jnp.logical_or(rr >= cc, cc >= blk).astype(BF16)
    causal = (lax.broadcasted_iota(jnp.int32, (blk, blk), 1) < lax.broadcasted_iota(jnp.int32, (blk, blk), 0))
    scale = HEAD_DIM ** -0.5
    qs = [q_ref[0, :, h * HEAD_DIM:(h + 1) * HEAD_DIM] * scale for h in range(N_HEADS)]

    heads = range(N_HEADS)
    lanes = [slice(h * HEAD_DIM, (h + 1) * HEAD_DIM) for h in heads]

    def tiles(kb, diagonal, runs):
        rows = pl.ds(pl.multiple_of(kb * blk, blk), blk)
        zs = [_dot_nt(qs[h], k_ref[0, rows, lanes[h]]) for h in heads]
        lks = [-_softplus(z) for z in zs]
        if diagonal:
            lks = [jnp.where(causal, lk, 0.0) for lk in lks]
        his = [lk.astype(BF16) for lk in lks]
        los = [(lk - hi.astype(F32)).astype(BF16) for lk, hi in zip(lks, his)]
        ss = [_dot(hi, sums) + _dot(lo, sums) for hi, lo in zip(his, los)]
        if diagonal:
            logw = [jnp.where(causal, z + s[:, :blk], NEG_BIG) for z, s in zip(zs, ss)]
            totals = [s[:, blk:] for s in ss]
        else:
            logw = [z + s[:, :blk] + run for z, s, run in zip(zs, ss, runs)]
            totals = [run + s[:, blk:] for s, run in zip(ss, runs)]
        pvs = [_dot(jnp.exp(lw).astype(BF16), v_ref[0, rows, lanes[h]]) for h, lw in zip(heads, logw)]
        return pvs, totals

    accs, runs = tiles(qi, True, None)

    def alive(runs):
        top = functools.reduce(jnp.maximum, runs)[:, 0:1]
        return jnp.max(top, axis=0, keepdims=True)[0, 0] > SB_EXIT

    def cond(carry):
        j, live, _, _ = carry
        return jnp.logical_and(j < qi, live)

    def body(carry):
        j, _, accs, runs = carry
        pvs, runs = tiles(qi - 1 - j, False, runs)
        return j + 1, alive(runs), tuple(a + p for a, p in zip(accs, pvs)), tuple(runs)

    _, _, accs, _ = lax.while_loop(cond, body, (0, alive(runs), tuple(accs), tuple(runs)))
    for h in heads:
        o_ref[0, :, lanes[h]] = accs[h]


def stick_breaking(pd_bf16, blk):
    b, s, _ = pd_bf16.shape
    return pl.pallas_call(
        _sb_kernel,
        grid=(b, s // blk),
        in_specs=[
            pl.BlockSpec((1, blk, GROUP), lambda bi, i: (bi, i, 0)),
            pl.BlockSpec((1, s, GROUP), lambda bi, i: (bi, 0, 1)),
            pl.BlockSpec((1, s, GROUP), lambda bi, i: (bi, 0, 2)),
        ],
        out_specs=pl.BlockSpec((1, blk, GROUP), lambda bi, i: (bi, i, 0)),
        out_shape=jax.ShapeDtypeStruct((b, s, GROUP), F32),
        compiler_params=_params("arbitrary", "arbitrary"),
        name="stick_breaking",
    )(pd_bf16, pd_bf16, pd_bf16)


def _outproj_kernel(x_ref, ya_ref, yb_ref, yc_ref, yd_ref, gt_ref, w_ref, o_ref):
    y = jnp.concatenate([ya_ref[0], yb_ref[0], yc_ref[0], yd_ref[0]], axis=-1).astype(BF16)
    o_ref[0] = x_ref[0] + gt_ref[0] * _dot(y, w_ref[...])


def out_projection(x, ys, gt, w_bf16, tile):
    b, s, d = x.shape
    row = lambda bi, i: (bi, i, 0)
    return pl.pallas_call(
        _outproj_kernel,
        grid=(b, s // tile),
        in_specs=[pl.BlockSpec((1, tile, d), row)]
        + [pl.BlockSpec((1, tile, GROUP), row) for _ in ys]
        + [pl.BlockSpec((1, 1, d), lambda bi, i: (bi, 0, 0)), _const_spec(w_bf16.shape)],
        out_specs=pl.BlockSpec((1, tile, d), row),
        out_shape=jax.ShapeDtypeStruct((b, s, d), F32),
        compiler_params=_params("arbitrary", "arbitrary"),
        name="out_projection",
    )(x, *ys, gt, w_bf16)


def _ffn_kernel(x_ref, xh_ref, sc_ref, sh_ref, gt_ref, g_ref, wv_ref, wg_ref, cwv_ref, cwg_ref,
                cbv_ref, cbg_ref, wd_ref, o_ref, uv_ref, ug_ref):
    i = pl.program_id(1)
    t = x_ref.shape[1]
    x = x_ref[0]
    xe = jnp.concatenate([xh_ref[0], x], axis=0)
    h = _modulated_norm(xe, g_ref[...], sc_ref[0], sh_ref[0]).astype(BF16)
    first = i == 0

    def conv(u_ref, slot, u, cw, cb):
        u_ref[slot, 0:HALO, :] = jnp.where(first, 0.0, u[0:HALO])
        u_ref[slot, HALO:, :] = u[HALO:]
        out = cb
        for j in range(FFN_CONV):
            out = out + cw[j:j + 1, :] * u_ref[slot, pl.ds(HALO - (FFN_CONV - 1) + j, t), :]
        return out

    acc = None
    for f in range(wv_ref.shape[0]):
        slot = f % 2
        val = conv(uv_ref, slot, _dot(h, wv_ref[f]), cwv_ref[f], cbv_ref[f])
        gate = conv(ug_ref, slot, _dot(h, wg_ref[f]), cwg_ref[f], cbg_ref[f])
        part = _dot((val * _gelu_tanh(gate)).astype(BF16), wd_ref[f])
        acc = part if acc is None else acc + part
    o_ref[0] = x + gt_ref[0] * acc


def conv_ffn(x, sc, sh, gt, g, w_up, conv_w, conv_b, w_down, tile, fchunk):
    b, s, d = x.shape
    f = w_down.shape[0]
    nf = f // fchunk
    row = lambda bi, i: (bi, i, 0)
    vec = lambda bi, i: (bi, 0, 0)
    halo = lambda bi, i: (bi, jnp.maximum(i * (tile // HALO) - 1, 0), 0)
    cols = lambda w: w.reshape(w.shape[0], nf, fchunk).transpose(1, 0, 2)
    wv = cols(w_up[:, :f]).astype(BF16)
    wg = cols(w_up[:, f:]).astype(BF16)
    wd = w_down.reshape(nf, fchunk, d).astype(BF16)
    cwv, cwg = cols(conv_w[:, :f]), cols(conv_w[:, f:])
    cbv, cbg = cols(conv_b[None, :f]), cols(conv_b[None, f:])
    return pl.pallas_call(
        _ffn_kernel,
        grid=(b, s // tile),
        in_specs=[
            pl.BlockSpec((1, tile, d), row),
            pl.BlockSpec((1, HALO, d), halo),
            pl.BlockSpec((1, 1, d), vec),
            pl.BlockSpec((1, 1, d), vec),
            pl.BlockSpec((1, 1, d), vec),
            _const_spec((1, d)),
            _const_spec(wv.shape),
            _const_spec(wg.shape),
            _const_spec(cwv.shape),
            _const_spec(cwg.shape),
            _const_spec(cbv.shape),
            _const_spec(cbg.shape),
            _const_spec(wd.shape),
        ],
        out_specs=pl.BlockSpec((1, tile, d), row),
        out_shape=jax.ShapeDtypeStruct((b, s, d), F32),
        scratch_shapes=[
            pltpu.VMEM((2, tile + HALO, fchunk), F32),
            pltpu.VMEM((2, tile + HALO, fchunk), F32),
        ],
        compiler_params=_params("arbitrary", "arbitrary"),
        name="conv_ffn",
    )(x, x, sc, sh, gt, g.reshape(1, d), wv, wg, cwv, cwg, cbv, cbg, wd)


def kernel(x, c, ada_w, ada_b, norm1_g, norm2_g, w_in, w_out, attn_q_gain, attn_k_gain, attn_rel_bias, lru_conv_w, lru_conv_b, lru_ra_w, lru_ra_b, lru_ri_w, lru_ri_b, lru_lambda, rwkv_mu, rwkv_w0, rwkv_w2, rwkv_a0, rwkv_a2, rwkv_g2, rwkv_k_k, rwkv_k_a, rwkv_r_k, rwkv_ln_w, rwkv_ln_b, ffn_up, ffn_conv_w, ffn_conv_b, ffn_down):
    depth = ada_w.shape[0]
    bsz, seq, d = x.shape
    p_rwkv = rwkv_mu.shape[1]
    widths = (3 * GROUP, 2 * GROUP, p_rwkv, 3 * GROUP)
    tile = min(SEQ_TILE, seq)
    qb = min(ATTN_QB, seq)
    mods = ada_modulation(c, ada_w, ada_b)
    for l in range(depth):
        sh_m, sc_m, gt_m, sh_f, sc_f, gt_f = [
            mods[l, :, j * d:(j + 1) * d].reshape(bsz, 1, d) for j in range(6)]
        pa, pb, pc, pd = in_projection(x, sc_m, sh_m, norm1_g[l], w_in[l].astype(BF16),
                                       attn_q_gain[l], attn_k_gain[l], widths, tile)
        y_a = chunk_attention(pa, attention_bias(attn_rel_bias[l], qb), qb)
        y_b = rglru(pb, lru_conv_w[l], lru_conv_b[l], lru_ra_w[l], lru_ra_b[l],
                    lru_ri_w[l], lru_ri_b[l], lru_lambda[l], tile)
        y_c = rwkv7(pc, rwkv_mu[l], rwkv_w0[l], rwkv_w2[l], rwkv_a0[l], rwkv_a2[l], rwkv_g2[l],
                    rwkv_k_k[l], rwkv_k_a[l], rwkv_r_k[l], rwkv_ln_w[l], rwkv_ln_b[l], tile)
        y_d = stick_breaking(pd, min(SB_BLOCK, seq))
        x = out_projection(x, (y_a, y_b, y_c, y_d), gt_m, w_out[l].astype(BF16), tile)
        x = conv_ffn(x, sc_f, sh_f, gt_f, norm2_g[l], ffn_up[l], ffn_conv_w[l], ffn_conv_b[l],
                     ffn_down[l], tile, FFN_CHUNK)
    return x
```

```python
import functools

import jax
import jax.numpy as jnp
from jax import lax
from jax.experimental import pallas as pl
from jax.experimental.pallas import tpu as pltpu

F32 = jnp.float32
BF16 = jnp.bfloat16
HI = lax.Precision.HIGHEST

HEAD_DIM = 64
N_HEADS = 4
GROUP = HEAD_DIM * N_HEADS
CHUNK = 64
ATTN_LEFT_CHUNKS = 8
ATTN_MAX_REL = 128
LRU_CONV = 4
LRU_C = 8.0
RWKV_LN_EPS = 64e-5
FFN_CONV = 3
NORM_EPS = 1e-6
HALO = 8
NEG_BIG = -1e30
SB_EXIT = -104.0

VMEM_LIMIT = 56 * 1024 * 1024

ATTN_QB = 256
SEQ_TILE = 512
SB_BLOCK = 128
RWKV_CHUNK = 64
RWKV_GROUP = 4
RWKV_SEQS = 2
FFN_CHUNK = 256


def _params(*sem):
    return pltpu.CompilerParams(dimension_semantics=sem, vmem_limit_bytes=VMEM_LIMIT)


def _sigmoid(x):
    return 1.0 / (1.0 + jnp.exp(-x))


def _softplus(x):
    return jnp.maximum(x, 0.0) + jnp.log(1.0 + jnp.exp(-jnp.abs(x)))


def _gelu_tanh(x):
    return 0.5 * x * (1.0 + jnp.tanh(0.7978845608028654 * (x + 0.044715 * x * x * x)))


def _dot_nt(a, b, **kw):
    return lax.dot_general(a, b, (((1,), (1,)), ((), ())), preferred_element_type=F32, **kw)


def _dot_tn(a, b, **kw):
    return lax.dot_general(a, b, (((0,), (0,)), ((), ())), preferred_element_type=F32, **kw)


def _dot(a, b, **kw):
    return jnp.dot(a, b, preferred_element_type=F32, **kw)


def _interleave(*stage_generators):
    live = list(stage_generators)
    done = object()
    while live:
        for gen in list(live):
            if next(gen, done) is done:
                live.remove(gen)


def _const_spec(shape):
    nd = len(shape)
    return pl.BlockSpec(shape, lambda *_: (0,) * nd)


def _ada_kernel(c_ref, w_ref, b_ref, o_ref):
    c = c_ref[...]
    s = c * _sigmoid(c)
    o_ref[0] = _dot(s, w_ref[0], precision=HI) + b_ref[0]


def ada_modulation(c, ada_w, ada_b):
    nl, d, n = ada_w.shape
    b = c.shape[0]
    rows = 8
    cp = jnp.zeros((rows, d), F32).at[:b].set(c)
    tn = 1536
    out = pl.pallas_call(
        _ada_kernel,
        grid=(nl, n // tn),
        in_specs=[
            pl.BlockSpec((rows, d), lambda l, j: (0, 0)),
            pl.BlockSpec((1, d, tn), lambda l, j: (l, 0, j)),
            pl.BlockSpec((1, 1, tn), lambda l, j: (l, 0, j)),
        ],
        out_specs=pl.BlockSpec((1, rows, tn), lambda l, j: (l, 0, j)),
        out_shape=jax.ShapeDtypeStruct((nl, rows, n), F32),
        compiler_params=_params("arbitrary", "arbitrary"),
        name="ada_modulation",
    )(cp, ada_w, ada_b.reshape(nl, 1, n))
    return out[:, :b]


def _modulated_norm(x, g, sc, sh):
    ms = jnp.mean(x * x, axis=-1, keepdims=True)
    return (x * lax.rsqrt(ms + NORM_EPS) * g) * (1.0 + sc) + sh


def _inproj_kernel(x_ref, sc_ref, sh_ref, g_ref, w_ref, qkg_ref, oa_ref, ob_ref, oc_ref, od_ref, *, cuts):
    h = _modulated_norm(x_ref[0], g_ref[...], sc_ref[0], sh_ref[0]).astype(BF16)
    c0, c1, c2, c3 = cuts
    pa = _dot(h, w_ref[:, 0:c0])
    ob_ref[0] = _dot(h, w_ref[:, c0:c1])
    oc_ref[0] = _dot(h, w_ref[:, c1:c2])
    od_ref[0] = _dot(h, w_ref[:, c2:c3]).astype(BF16)
    nqk = 2 * GROUP
    qk = pa[:, :nqk]
    sq = qk * qk
    hi = sq.astype(BF16)
    lo = (sq - hi.astype(F32)).astype(BF16)
    hr = lax.broadcasted_iota(jnp.int32, (nqk, nqk), 0) // HEAD_DIM
    hc = lax.broadcasted_iota(jnp.int32, (nqk, nqk), 1) // HEAD_DIM
    same_head = (hr == hc).astype(BF16)
    ms = (_dot(hi, same_head) + _dot(lo, same_head)) * (1.0 / HEAD_DIM)
    oa_ref[0, :, :nqk] = (qk * lax.rsqrt(ms + NORM_EPS) * qkg_ref[...]).astype(BF16)
    oa_ref[0, :, nqk:] = pa[:, nqk:].astype(BF16)


def in_projection(x, sc, sh, g, w_bf16, q_gain, k_gain, widths, tile):
    b, s, d = x.shape
    cuts = tuple(int(sum(widths[:i + 1])) for i in range(4))
    row = lambda bi, i: (bi, i, 0)
    vec = lambda bi, i: (bi, 0, 0)
    dts = (BF16, F32, F32, BF16)
    qkg = jnp.concatenate([jnp.tile(q_gain.astype(F32), N_HEADS) * (HEAD_DIM ** -0.5),
                           jnp.tile(k_gain.astype(F32), N_HEADS)]).reshape(1, 2 * GROUP)
    return pl.pallas_call(
        functools.partial(_inproj_kernel, cuts=cuts),
        grid=(b, s // tile),
        in_specs=[
            pl.BlockSpec((1, tile, d), row),
            pl.BlockSpec((1, 1, d), vec),
            pl.BlockSpec((1, 1, d), vec),
            _const_spec((1, d)),
            _const_spec(w_bf16.shape),
            _const_spec((1, 2 * GROUP)),
        ],
        out_specs=[pl.BlockSpec((1, tile, w), row) for w in widths],
        out_shape=[jax.ShapeDtypeStruct((b, s, w), dt) for w, dt in zip(widths, dts)],
        compiler_params=_params("arbitrary", "arbitrary"),
        name="in_projection",
    )(x, sc, sh, g.reshape(1, d), w_bf16, qkg)


def _attn_kernel(*refs, nb):
    q_ref, k_refs, v_refs = refs[0], refs[1:1 + nb], refs[1 + nb:1 + 2 * nb]
    bias_ref, o_ref = refs[1 + 2 * nb:]
    heads = range(N_HEADS)
    lane = lax.broadcasted_iota(jnp.int32, (1, 2 * HEAD_DIM), 1)
    keep = [lane < HEAD_DIM, lane >= HEAD_DIM]
    groups = [slice(g * 2 * HEAD_DIM, (g + 1) * 2 * HEAD_DIM) for g in range(N_HEADS // 2)]
    q2 = [q_ref[0, :, g] for g in groups]
    k2 = [jnp.concatenate([r[0, :, g] for r in k_refs], axis=0) for g in groups]
    v2 = [jnp.concatenate([r[0, :, g] for r in v_refs], axis=0) for g in groups]
    qs = [jnp.where(keep[h % 2], q2[h // 2], jnp.zeros_like(q2[0])) for h in heads]
    ss = [_dot_nt(qs[h], k2[h // 2]) + bias_ref[0, h] for h in heads]
    ps = [jnp.exp(s - jnp.max(s, axis=-1, keepdims=True)) for s in ss]
    dens = [jnp.sum(p, axis=-1, keepdims=True) for p in ps]
    outs = [_dot(ps[h].astype(BF16), v2[h // 2]) / dens[h] for h in heads]
    for g, lanes in enumerate(groups):
        o_ref[0, :, lanes] = jnp.where(keep[0], outs[2 * g], outs[2 * g + 1])


def attention_bias(rel_bias, qb):
    nh = rel_bias.shape[0]
    nb = 1 + ATTN_LEFT_CHUNKS * CHUNK // qb
    nk = nb * qb
    back = nk - qb
    qi = jnp.arange(qb)[:, None]
    kj = jnp.arange(nk)[None, :]
    qc = (qi + back) // CHUNK
    kc = kj // CHUNK
    valid = jnp.logical_and(kc <= qc, kc >= qc - ATTN_LEFT_CHUNKS)
    m = jnp.arange(nk + 1)
    rel = jnp.where(m < back + CHUNK,
                    jnp.clip(back - m, -ATTN_MAX_REL, ATTN_MAX_REL) + ATTN_MAX_REL, 2 * ATTN_MAX_REL)
    vec = rel_bias.astype(F32)[:, rel]
    toeplitz = jnp.tile(vec, (1, qb))[:, :qb * nk].reshape(nh, qb, nk)
    first_key = (nb - 1 - jnp.arange(nb))[:, None, None, None] * qb
    live = jnp.logical_and(valid[None, None], kj[None, None] >= first_key)
    return jnp.where(live, toeplitz[None], NEG_BIG)


def chunk_attention(pa, bias, qb):
    b, s, _ = pa.shape
    nb = bias.shape[0]
    blk = (1, qb, GROUP)

    def window(col):
        return [pl.BlockSpec(blk, functools.partial(
            lambda bi, i, back: (bi, jnp.maximum(i - back, 0), col), back=back))
            for back in range(nb - 1, -1, -1)]

    return pl.pallas_call(
        functools.partial(_attn_kernel, nb=nb),
        grid=(b, s // qb),
        in_specs=[pl.BlockSpec(blk, lambda bi, i: (bi, i, 0))] + window(1) + window(2) + [
            pl.BlockSpec((1,) + bias.shape[1:], lambda bi, i: (jnp.minimum(i, nb - 1), 0, 0, 0)),
        ],
        out_specs=pl.BlockSpec(blk, lambda bi, i: (bi, i, 0)),
        out_shape=jax.ShapeDtypeStruct((b, s, GROUP), F32),
        compiler_params=_params("arbitrary", "arbitrary"),
        name="chunk_attention",
    )(*([pa] * (1 + 2 * nb)), bias)


def _lru_kernel(p_ref, cw_ref, cb_ref, raw_ref, rab_ref, riw_ref, rib_ref, lam_ref, o_ref,
                xpad_ref, h_ref):
    i = pl.program_id(1)
    t = p_ref.shape[1]

    @pl.when(i == 0)
    def _():
        xpad_ref[0:HALO, :] = jnp.zeros((HALO, GROUP), F32)
        h_ref[...] = jnp.zeros_like(h_ref)

    @pl.when(i > 0)
    def _():
        xpad_ref[0:HALO, :] = xpad_ref[t:t + HALO, :]

    xpad_ref[HALO:HALO + t, :] = p_ref[0, :, 0:GROUP]
    gate = p_ref[0, :, GROUP:2 * GROUP]
    xc = cb_ref[...]
    for j in range(LRU_CONV):
        xc = xc + cw_ref[j:j + 1, :] * xpad_ref[pl.ds(HALO - (LRU_CONV - 1) + j, t), :]
    xcb = xc.astype(BF16)
    r_gate = _sigmoid(_dot(xcb, raw_ref[...]) + rab_ref[...])
    i_gate = _sigmoid(_dot(xcb, riw_ref[...]) + rib_ref[...])
    log_a = (-LRU_C * r_gate) * _softplus(-lam_ref[...])
    a = jnp.exp(log_a)
    u = jnp.sqrt(-jnp.tanh(log_a) * (a * a + 1.0)) * (i_gate * xc)
    rows = lax.broadcasted_iota(jnp.int32, (t, 1), 0)
    k = 1
    while k < t:
        keep = rows >= k
        a_sh = jnp.where(keep, pltpu.roll(a, k, 0), 1.0)
        u_sh = jnp.where(keep, pltpu.roll(u, k, 0), 0.0)
        u = a * u_sh + u
        a = a * a_sh
        k *= 2
    hh = a * h_ref[...] + u
    h_ref[...] = hh[t - 1:t, :]
    o_ref[0] = hh * _gelu_tanh(gate)


def _block_diag(w):
    h, n, _ = w.shape
    eye = jnp.eye(h, dtype=w.dtype)
    return (eye[:, None, :, None] * w[:, :, None, :]).reshape(h * n, h * n)


def rglru(pb, conv_w, conv_b, ra_w, ra_b, ri_w, ri_b, lam, tile):
    b, s, _ = pb.shape
    row = lambda bi, i: (bi, i, 0)
    vec = lambda a: a.reshape(1, GROUP)
    return pl.pallas_call(
        _lru_kernel,
        grid=(b, s // tile),
        in_specs=[
            pl.BlockSpec((1, tile, 2 * GROUP), row),
            _const_spec((LRU_CONV, GROUP)),
            _const_spec((1, GROUP)),
            _const_spec((GROUP, GROUP)),
            _const_spec((1, GROUP)),
            _const_spec((GROUP, GROUP)),
            _const_spec((1, GROUP)),
            _const_spec((1, GROUP)),
        ],
        out_specs=pl.BlockSpec((1, tile, GROUP), row),
        out_shape=jax.ShapeDtypeStruct((b, s, GROUP), F32),
        scratch_shapes=[pltpu.VMEM((tile + HALO, GROUP), F32), pltpu.VMEM((1, GROUP), F32)],
        compiler_params=_params("arbitrary", "arbitrary"),
        name="rglru",
    )(pb, conv_w, vec(conv_b), _block_diag(ra_w).astype(BF16), vec(ra_b),
      _block_diag(ri_w).astype(BF16), vec(ri_b), vec(lam))


def _rwkv_kernel(p_ref, mu_ref, w0_ref, w2_ref, a0_ref, a2_ref, g2_ref, kk_ref, ka_ref, rk_ref,
                 lnw_ref, lnb_ref, o_ref, ppad_ref, state_ref, y_ref):
    i = pl.program_id(1)
    nbat, t, width = p_ref.shape
    ch = RWKV_CHUNK
    g3 = 3 * GROUP

    @pl.when(i == 0)
    def _():
        ppad_ref[:, 0:HALO, :] = jnp.zeros((nbat, HALO, width), F32)
        state_ref[...] = jnp.zeros_like(state_ref)

    @pl.when(i > 0)
    def _():
        ppad_ref[:, 0:HALO, :] = ppad_ref[:, t:t + HALO, :]

    ppad_ref[:, HALO:HALO + t, :] = p_ref[...]
    p = jnp.concatenate([p_ref[b] for b in range(nbat)], axis=0)
    p_prev = jnp.concatenate([ppad_ref[b, pl.ds(HALO - 1, t), :] for b in range(nbat)], axis=0)
    p = p + (p_prev - p) * mu_ref[...]
    r = p[:, 0:GROUP]
    k = p[:, GROUP:2 * GROUP]
    v = p[:, 2 * GROUP:g3]
    low = p[:, g3:width]
    w = -_softplus(-(w0_ref[...] + _dot(jnp.tanh(low).astype(BF16), w2_ref[...]))) - 0.5
    a = _sigmoid(a0_ref[...] + _dot(low.astype(BF16), a2_ref[...]))
    out_gate = _dot(_sigmoid(low).astype(BF16), g2_ref[...])

    hr = lax.broadcasted_iota(jnp.int32, (GROUP, GROUP), 0) // HEAD_DIM
    hc = lax.broadcasted_iota(jnp.int32, (GROUP, GROUP), 1) // HEAD_DIM
    head_ones = (hr == hc).astype(BF16)

    def split(x):
        hi = x.astype(BF16)
        return hi, (x - hi.astype(F32)).astype(BF16)

    def head_sum(x):
        hi, lo = split(x)
        return _dot(hi, head_ones) + _dot(lo, head_ones)

    kk = k * kk_ref[...]
    kk = kk / jnp.maximum(jnp.sqrt(head_sum(kk * kk)), 1e-12)
    k = k * (1.0 + (a - 1.0) * ka_ref[...])

    logd = -jnp.exp(w)
    kka = kk * a
    per_seq = t // ch
    nch = nbat * per_seq
    rows = [slice(c * ch, (c + 1) * ch) for c in range(nch)]
    pair = 2 * HEAD_DIM
    groups2 = [slice(g * pair, (g + 1) * pair) for g in range(N_HEADS // 2)]
    lane = lax.broadcasted_iota(jnp.int32, (1, pair), 1)
    first_head = lane < HEAD_DIM

    def blk(x, c, g):
        return x[rows[c], groups2[g]]

    def block_diag(x2):
        zero = jnp.zeros_like(x2)
        return jnp.concatenate([jnp.where(first_head, x2, zero), jnp.where(first_head, zero, x2)], axis=0)

    def diag_blocks(full):
        return jnp.where(first_head, full[:HEAD_DIM], full[HEAD_DIM:])

    rr = lax.broadcasted_iota(jnp.int32, (ch, ch), 0)
    cc = lax.broadcasted_iota(jnp.int32, (ch, ch), 1)
    tril = (rr >= cc).astype(BF16)
    er = lax.broadcasted_iota(jnp.int32, (ch, pair), 0)
    ec = jnp.bitwise_and(lax.broadcasted_iota(jnp.int32, (ch, pair), 1), ch - 1)
    eye2 = (er == ec).astype(F32)
    gr = lax.broadcasted_iota(jnp.int32, (2 * ch, 2 * pair), 0)
    gc = jnp.bitwise_and(lax.broadcasted_iota(jnp.int32, (2 * ch, 2 * pair), 1), ch - 1)
    gram_mask = jnp.logical_or(jnp.logical_and(gr < ch, gc < gr),
                               jnp.logical_and(gr >= ch, gc <= gr - ch))

    logd_hi, logd_lo = split(logd)
    cums = [_dot(tril, logd_hi[rs, :]) + _dot(tril, logd_lo[rs, :]) for rs in rows]
    cum = jnp.concatenate(cums, axis=0)
    cum_end = jnp.concatenate([jnp.broadcast_to(cm[ch - 1:ch, :], (ch, GROUP)) for cm in cums], axis=0)
    p_inv = jnp.exp(-cum)
    p_end = jnp.exp(cum_end - cum)
    a_t = (-kk * jnp.exp(cum - logd)).astype(BF16)
    b_t = (kka * p_inv).astype(BF16)
    k_t = (k * p_inv).astype(BF16)
    r_t = (r * jnp.exp(cum)).astype(BF16)
    b_h = (kka * p_end).astype(BF16)
    k_h = (k * p_end).astype(BF16)
    v_b = v.astype(BF16)
    decay_end = [jnp.exp(cm[ch - 1:ch, :]) for cm in cums]
    pairs = range(len(groups2))
    ready = {}

    def independent(chunks):
        prs = [(c, g) for c in chunks for g in pairs]
        grams = [jnp.where(gram_mask,
                           _dot_nt(jnp.concatenate([blk(a_t, c, g), blk(r_t, c, g)], axis=0),
                                   jnp.concatenate([block_diag(blk(b_t, c, g)), block_diag(blk(k_t, c, g))],
                                                   axis=0)), 0.0)
                 for c, g in prs]
        yield
        a_ak = [gm[:ch, pair:].astype(BF16) for gm in grams]
        g_bot = [gm[ch:].astype(BF16) for gm in grams]
        tinvs = [eye2 + gm[:ch, :pair] for gm in grams]
        npows = [gm[:ch, :pair].astype(BF16) for gm in grams]
        steps = 1
        while 2 * steps < ch:
            npows = [_dot(pw, block_diag(pw)).astype(BF16) for pw in npows]
            yield
            tinvs = [ti + _dot(ti.astype(BF16), block_diag(pw)) for ti, pw in zip(tinvs, npows)]
            yield
            steps *= 2
        ws = [_dot(ak, block_diag(blk(v_b, c, g))) for ak, (c, g) in zip(a_ak, prs)]
        gvs = [diag_blocks(_dot_tn(blk(v_b, c, g), blk(k_h, c, g))) for c, g in prs]
        for j, pr in enumerate(prs):
            ready[pr] = (g_bot[j], tinvs[j].astype(BF16), ws[j], gvs[j])

    states = [[state_ref[b, g] for g in pairs] for b in range(nbat)]

    def dependent(chunks):
        sts = states[chunks[0] // per_seq]
        for c in chunks:
            g_bot, tinv, w, gv = zip(*[ready[(c, g)] for g in pairs])
            stb = [block_diag(st.astype(BF16)) for st in sts]
            rhs = [_dot_nt(blk(a_t, c, g), stb[g]) + w[g] for g in pairs]
            yield
            us = [_dot(tinv[g], block_diag(rhs[g].astype(BF16))).astype(BF16) for g in pairs]
            yield
            ys = [_dot_nt(blk(r_t, c, g), stb[g])
                  + _dot(g_bot[g], jnp.concatenate([block_diag(us[g]), block_diag(blk(v_b, c, g))], axis=0))
                  for g in pairs]
            for g in pairs:
                sts[g] = (sts[g] * decay_end[c][:, groups2[g]]
                          + diag_blocks(_dot_tn(us[g], blk(b_h, c, g))) + gv[g])
                y_ref[rows[c], groups2[g]] = ys[g]
            yield

    waves = [[list(range(b * per_seq + c, b * per_seq + min(c + RWKV_GROUP, per_seq))) for b in range(nbat)]
             for c in range(0, per_seq, RWKV_GROUP)]
    _interleave(*[independent(grp) for grp in waves[0]])
    for done, coming in zip(waves, waves[1:]):
        _interleave(*([dependent(grp) for grp in done] + [independent(grp) for grp in coming]))
    _interleave(*[dependent(grp) for grp in waves[-1]])
    for b in range(nbat):
        for g in pairs:
            state_ref[b, g] = states[b][g]

    y = y_ref[...]
    mean = head_sum(y) * (1.0 / HEAD_DIM)
    yc = y - mean
    var = head_sum(yc * yc) * (1.0 / HEAD_DIM)
    y = yc * lax.rsqrt(var + RWKV_LN_EPS) * lnw_ref[...] + lnb_ref[...]
    y = y + head_sum(r * k * rk_ref[...]) * v
    y = y * out_gate
    for b in range(nbat):
        o_ref[b] = y[b * t:(b + 1) * t]


def _pad_rows(w, start, total):
    return jnp.zeros((total, w.shape[1]), w.dtype).at[start:start + w.shape[0]].set(w)


def rwkv7(pc, mu, w0, w2, a0, a2, g2, k_k, k_a, r_k, ln_w, ln_b, tile):
    b, s, width = pc.shape
    assert RWKV_CHUNK == HEAD_DIM, "the side-by-side head layout needs square per-head chunk matrices"
    low = width - 3 * GROUP
    dr, ir = w2.shape[0], a2.shape[0]
    nbat = RWKV_SEQS if b % RWKV_SEQS == 0 else 1
    row = lambda bi, i: (bi, i, 0)
    vec = lambda a: a.reshape(1, -1)
    return pl.pallas_call(
        _rwkv_kernel,
        grid=(b // nbat, s // tile),
        in_specs=[
            pl.BlockSpec((nbat, tile, width), row),
            _const_spec((1, width)),
            _const_spec((1, GROUP)),
            _const_spec((low, GROUP)),
            _const_spec((1, GROUP)),
            _const_spec((low, GROUP)),
            _const_spec((low, GROUP)),
            _const_spec((1, GROUP)),
            _const_spec((1, GROUP)),
            _const_spec((1, GROUP)),
            _const_spec((1, GROUP)),
            _const_spec((1, GROUP)),
        ],
        out_specs=pl.BlockSpec((nbat, tile, GROUP), row),
        out_shape=jax.ShapeDtypeStruct((b, s, GROUP), F32),
        scratch_shapes=[
            pltpu.VMEM((nbat, tile + HALO, width), F32),
            pltpu.VMEM((nbat, N_HEADS // 2, HEAD_DIM, 2 * HEAD_DIM), F32),
            pltpu.VMEM((nbat * tile, GROUP), F32),
        ],
        compiler_params=_params("arbitrary", "arbitrary"),
        name="rwkv7",
    )(pc, vec(mu), vec(w0), _pad_rows(w2, 0, low).astype(BF16), vec(a0),
      _pad_rows(a2, dr, low).astype(BF16), _pad_rows(g2, dr + ir, low).astype(BF16),
      vec(k_k), vec(k_a), vec(r_k), vec(ln_w), vec(ln_b))


def _sb_kernel(q_ref, k_ref, v_ref, o_ref):
    qi = pl.program_id(1)
    blk = q_ref.shape[1]
    rr = lax.broadcasted_iota(jnp.int32, (blk, 2 * blk), 0)
    cc = lax.broadcasted_iota(jnp.int32, (blk, 2 * blk), 1)
    sums = jnp.logical_or(rr >= cc, cc >= blk).astype(BF16)
    causal = (lax.broadcasted_iota(jnp.int32, (blk, blk), 1) < lax.broadcasted_iota(jnp.int32, (blk, blk), 0))
    scale = HEAD_DIM ** -0.5
    qs = [q_ref[0, :, h * HEAD_DIM:(h + 1) * HEAD_DIM] * scale for h in range(N_HEADS)]

    heads = range(N_HEADS)
    lanes = [slice(h * HEAD_DIM, (h + 1) * HEAD_DIM) for h in heads]

    def tiles(kb, diagonal, runs):
        rows = pl.ds(pl.multiple_of(kb * blk, blk), blk)
        zs = [_dot_nt(qs[h], k_ref[0, rows, lanes[h]]) for h in heads]
        lks = [-_softplus(z) for z in zs]
        if diagonal:
            lks = [jnp.where(causal, lk, 0.0) for lk in lks]
        his = [lk.astype(BF16) for lk in lks]
        los = [(lk - hi.astype(F32)).astype(BF16) for lk, hi in zip(lks, his)]
        ss = [_dot(hi, sums) + _dot(lo, sums) for hi, lo in zip(his, los)]
        if diagonal:
            logw = [jnp.where(causal, z + s[:, :blk], NEG_BIG) for z, s in zip(zs, ss)]
            totals = [s[:, blk:] for s in ss]
        else:
            logw = [z + s[:, :blk] + run for z, s, run in zip(zs, ss, runs)]
            totals = [run + s[:, blk:] for s, run in zip(ss, runs)]
        pvs = [_dot(jnp.exp(lw).astype(BF16), v_ref[0, rows, lanes[h]]) for h, lw in zip(heads, logw)]
        return pvs, totals

    accs, runs = tiles(qi, True, None)

    def alive(runs):
        top = functools.reduce(jnp.maximum, runs)[:, 0:1]
        return jnp.max(top, axis=0, keepdims=True)[0, 0] > SB_EXIT

    def cond(carry):
        j, live, _, _ = carry
        return jnp.logical_and(j < qi, live)

    def body(carry):
        j, _, accs, runs = carry
        pvs, runs = tiles(qi - 1 - j, False, runs)
        return j + 1, alive(runs), tuple(a + p for a, p in zip(accs, pvs)), tuple(runs)

    _, _, accs, _ = lax.while_loop(cond, body, (0, alive(runs), tuple(accs), tuple(runs)))
    for h in heads:
        o_ref[0, :, lanes[h]] = accs[h]


def stick_breaking(pd_bf16, blk):
    b, s, _ = pd_bf16.shape
    return pl.pallas_call(
        _sb_kernel,
        grid=(b, s // blk),
        in_specs=[
            pl.BlockSpec((1, blk, GROUP), lambda bi, i: (bi, i, 0)),
            pl.BlockSpec((1, s, GROUP), lambda bi, i: (bi, 0, 1)),
            pl.BlockSpec((1, s, GROUP), lambda bi, i: (bi, 0, 2)),
        ],
        out_specs=pl.BlockSpec((1, blk, GROUP), lambda bi, i: (bi, i, 0)),
        out_shape=jax.ShapeDtypeStruct((b, s, GROUP), F32),
        compiler_params=_params("arbitrary", "arbitrary"),
        name="stick_breaking",
    )(pd_bf16, pd_bf16, pd_bf16)


def _outproj_kernel(x_ref, ya_ref, yb_ref, yc_ref, yd_ref, gt_ref, w_ref, o_ref):
    y = jnp.concatenate([ya_ref[0], yb_ref[0], yc_ref[0], yd_ref[0]], axis=-1).astype(BF16)
    o_ref[0] = x_ref[0] + gt_ref[0] * _dot(y, w_ref[...])


def out_projection(x, ys, gt, w_bf16, tile):
    b, s, d = x.shape
    row = lambda bi, i: (bi, i, 0)
    return pl.pallas_call(
        _outproj_kernel,
        grid=(b, s // tile),
        in_specs=[pl.BlockSpec((1, tile, d), row)]
        + [pl.BlockSpec((1, tile, GROUP), row) for _ in ys]
        + [pl.BlockSpec((1, 1, d), lambda bi, i: (bi, 0, 0)), _const_spec(w_bf16.shape)],
        out_specs=pl.BlockSpec((1, tile, d), row),
        out_shape=jax.ShapeDtypeStruct((b, s, d), F32),
        compiler_params=_params("arbitrary", "arbitrary"),
        name="out_projection",
    )(x, *ys, gt, w_bf16)


def _ffn_kernel(x_ref, xh_ref, sc_ref, sh_ref, gt_ref, g_ref, wv_ref, wg_ref, cwv_ref, cwg_ref,
                cbv_ref, cbg_ref, wd_ref, o_ref, uv_ref, ug_ref, act_ref):
    i = pl.program_id(1)
    t = x_ref.shape[1]
    x = x_ref[0]
    xe = jnp.concatenate([xh_ref[0], x], axis=0)
    h = _modulated_norm(xe, g_ref[...], sc_ref[0], sh_ref[0]).astype(BF16)
    first = i == 0

    def up(f):
        for u_ref, w_ref in ((uv_ref, wv_ref), (ug_ref, wg_ref)):
            u = _dot(h, w_ref[f])
            u_ref[f % 2, 0:HALO, :] = jnp.where(first, 0.0, u[0:HALO])
            u_ref[f % 2, HALO:, :] = u[HALO:]

    def conv(u_ref, f, cw, cb):
        out = cb
        for j in range(FFN_CONV):
            out = out + cw[j:j + 1, :] * u_ref[f % 2, pl.ds(HALO - (FFN_CONV - 1) + j, t), :]
        return out

    nf = wv_ref.shape[0]
    fc = wv_ref.shape[2]
    up(0)
    for f in range(nf):
        if f + 1 < nf:
            up(f + 1)
        val = conv(uv_ref, f, cwv_ref[f], cbv_ref[f])
        gate = conv(ug_ref, f, cwg_ref[f], cbg_ref[f])
        act_ref[:, f * fc:(f + 1) * fc] = (val * _gelu_tanh(gate)).astype(BF16)
    o_ref[0] = x + gt_ref[0] * _dot(act_ref[...], wd_ref[...])


def conv_ffn(x, sc, sh, gt, g, w_up, conv_w, conv_b, w_down, tile, fchunk):
    b, s, d = x.shape
    f = w_down.shape[0]
    nf = f // fchunk
    row = lambda bi, i: (bi, i, 0)
    vec = lambda bi, i: (bi, 0, 0)
    halo = lambda bi, i: (bi, jnp.maximum(i * (tile // HALO) - 1, 0), 0)
    cols = lambda w: w.reshape(w.shape[0], nf, fchunk).transpose(1, 0, 2)
    wv = cols(w_up[:, :f]).astype(BF16)
    wg = cols(w_up[:, f:]).astype(BF16)
    wd = w_down.astype(BF16)
    cwv, cwg = cols(conv_w[:, :f]), cols(conv_w[:, f:])
    cbv, cbg = cols(conv_b[None, :f]), cols(conv_b[None, f:])
    return pl.pallas_call(
        _ffn_kernel,
        grid=(b, s // tile),
        in_specs=[
            pl.BlockSpec((1, tile, d), row),
            pl.BlockSpec((1, HALO, d), halo),
            pl.BlockSpec((1, 1, d), vec),
            pl.BlockSpec((1, 1, d), vec),
            pl.BlockSpec((1, 1, d), vec),
            _const_spec((1, d)),
            _const_spec(wv.shape),
            _const_spec(wg.shape),
            _const_spec(cwv.shape),
            _const_spec(cwg.shape),
            _const_spec(cbv.shape),
            _const_spec(cbg.shape),
            _const_spec(wd.shape),
        ],
        out_specs=pl.BlockSpec((1, tile, d), row),
        out_shape=jax.ShapeDtypeStruct((b, s, d), F32),
        scratch_shapes=[
            pltpu.VMEM((2, tile + HALO, fchunk), F32),
            pltpu.VMEM((2, tile + HALO, fchunk), F32),
            pltpu.VMEM((tile, f), BF16),
        ],
        compiler_params=_params("arbitrary", "arbitrary"),
        name="conv_ffn",
    )(x, x, sc, sh, gt, g.reshape(1, d), wv, wg, cwv, cwg, cbv, cbg, wd)


def kernel(x, c, ada_w, ada_b, norm1_g, norm2_g, w_in, w_out, attn_q_gain, attn_k_gain, attn_rel_bias, lru_conv_w, lru_conv_b, lru_ra_w, lru_ra_b, lru_ri_w, lru_ri_b, lru_lambda, rwkv_mu, rwkv_w0, rwkv_w2, rwkv_a0, rwkv_a2, rwkv_g2, rwkv_k_k, rwkv_k_a, rwkv_r_k, rwkv_ln_w, rwkv_ln_b, ffn_up, ffn_conv_w, ffn_conv_b, ffn_down):
    depth = ada_w.shape[0]
    bsz, seq, d = x.shape
    p_rwkv = rwkv_mu.shape[1]
    widths = (3 * GROUP, 2 * GROUP, p_rwkv, 3 * GROUP)
    tile = min(SEQ_TILE, seq)
    qb = min(ATTN_QB, seq)
    mods = ada_modulation(c, ada_w, ada_b)
    for l in range(depth):
        sh_m, sc_m, gt_m, sh_f, sc_f, gt_f = [
            mods[l, :, j * d:(j + 1) * d].reshape(bsz, 1, d) for j in range(6)]
        pa, pb, pc, pd = in_projection(x, sc_m, sh_m, norm1_g[l], w_in[l].astype(BF16),
                                       attn_q_gain[l], attn_k_gain[l], widths, tile)
        y_a = chunk_attention(pa, attention_bias(attn_rel_bias[l], qb), qb)
        y_b = rglru(pb, lru_conv_w[l], lru_conv_b[l], lru_ra_w[l], lru_ra_b[l],
                    lru_ri_w[l], lru_ri_b[l], lru_lambda[l], tile)
        y_c = rwkv7(pc, rwkv_mu[l], rwkv_w0[l], rwkv_w2[l], rwkv_a0[l], rwkv_a2[l], rwkv_g2[l],
                    rwkv_k_k[l], rwkv_k_a[l], rwkv_r_k[l], rwkv_ln_w[l], rwkv_ln_b[l], tile)
        y_d = stick_breaking(pd, min(SB_BLOCK, seq))
        x = out_projection(x, (y_a, y_b, y_c, y_d), gt_m, w_out[l].astype(BF16), tile)
        x = conv_ffn(x, sc_f, sh_f, gt_f, norm2_g[l], ffn_up[l], ffn_conv_w[l], ffn_conv_b[l],
                     ffn_down[l], tile, FFN_CHUNK)
    return x
```

```python
import functools

import jax
import jax.numpy as jnp
from jax import lax
from jax.experimental import pallas as pl
from jax.experimental.pallas import tpu as pltpu

F32 = jnp.float32
BF16 = jnp.bfloat16
HI = lax.Precision.HIGHEST

HEAD_DIM = 64
N_HEADS = 4
GROUP = HEAD_DIM * N_HEADS
CHUNK = 64
ATTN_LEFT_CHUNKS = 8
ATTN_MAX_REL = 128
LRU_CONV = 4
LRU_C = 8.0
RWKV_LN_EPS = 64e-5
FFN_CONV = 3
NORM_EPS = 1e-6
HALO = 8
NEG_BIG = -1e30
SB_EXIT = -104.0

VMEM_LIMIT = 56 * 1024 * 1024

ATTN_QB = 256
SEQ_TILE = 512
MATMUL_TILE = 1024
SB_BLOCK = 128
SB_FIRST_BLOCKS = 3
SB_LOOP_BLOCKS = 2
RWKV_CHUNK = 64
RWKV_GROUP = 4
RWKV_SEQS = 2
FFN_CHUNK = 256


def _params(*sem):
    return pltpu.CompilerParams(dimension_semantics=sem, vmem_limit_bytes=VMEM_LIMIT)


def _sigmoid(x):
    return 1.0 / (1.0 + jnp.exp(-x))


def _softplus(x):
    return jnp.maximum(x, 0.0) + jnp.log(1.0 + jnp.exp(-jnp.abs(x)))


def _gelu_tanh(x):
    return 0.5 * x * (1.0 + jnp.tanh(0.7978845608028654 * (x + 0.044715 * x * x * x)))


def _dot_nt(a, b, **kw):
    return lax.dot_general(a, b, (((1,), (1,)), ((), ())), preferred_element_type=F32, **kw)


def _dot_tn(a, b, **kw):
    return lax.dot_general(a, b, (((0,), (0,)), ((), ())), preferred_element_type=F32, **kw)


def _dot(a, b, **kw):
    return jnp.dot(a, b, preferred_element_type=F32, **kw)


def _interleave(*stage_generators):
    live = list(stage_generators)
    done = object()
    while live:
        for gen in list(live):
            if next(gen, done) is done:
                live.remove(gen)


def _const_spec(shape):
    nd = len(shape)
    return pl.BlockSpec(shape, lambda *_: (0,) * nd, pipeline_mode=pl.Buffered(1))


def _ada_kernel(c_ref, w_ref, b_ref, o_ref):
    c = c_ref[...]
    s = c * _sigmoid(c)
    o_ref[0] = _dot(s, w_ref[0], precision=HI) + b_ref[0]


def ada_modulation(c, ada_w, ada_b):
    nl, d, n = ada_w.shape
    b = c.shape[0]
    rows = 8
    cp = jnp.zeros((rows, d), F32).at[:b].set(c)
    tn = 1536
    out = pl.pallas_call(
        _ada_kernel,
        grid=(nl, n // tn),
        in_specs=[
            pl.BlockSpec((rows, d), lambda l, j: (0, 0)),
            pl.BlockSpec((1, d, tn), lambda l, j: (l, 0, j)),
            pl.BlockSpec((1, 1, tn), lambda l, j: (l, 0, j)),
        ],
        out_specs=pl.BlockSpec((1, rows, tn), lambda l, j: (l, 0, j)),
        out_shape=jax.ShapeDtypeStruct((nl, rows, n), F32),
        compiler_params=_params("arbitrary", "arbitrary"),
        name="ada_modulation",
    )(cp, ada_w, ada_b.reshape(nl, 1, n))
    return out[:, :b]


def _modulated_norm(x, g, sc, sh):
    ms = jnp.mean(x * x, axis=-1, keepdims=True)
    return (x * lax.rsqrt(ms + NORM_EPS) * g) * (1.0 + sc) + sh


def _inproj_kernel(x_ref, sc_ref, sh_ref, g_ref, w_ref, qkg_ref, oa_ref, ob_ref, oc_ref, od_ref, *, cuts):
    h = _modulated_norm(x_ref[0], g_ref[...], sc_ref[0], sh_ref[0]).astype(BF16)
    c0, c1, c2, c3 = cuts
    pa = _dot(h, w_ref[:, 0:c0])
    ob_ref[0] = _dot(h, w_ref[:, c0:c1])
    oc_ref[0] = _dot(h, w_ref[:, c1:c2])
    od_ref[0] = _dot(h, w_ref[:, c2:c3]).astype(BF16)
    nqk = 2 * GROUP
    qk = pa[:, :nqk]
    sq = qk * qk
    hi = sq.astype(BF16)
    lo = (sq - hi.astype(F32)).astype(BF16)
    hr = lax.broadcasted_iota(jnp.int32, (nqk, nqk), 0) // HEAD_DIM
    hc = lax.broadcasted_iota(jnp.int32, (nqk, nqk), 1) // HEAD_DIM
    same_head = (hr == hc).astype(BF16)
    ms = (_dot(hi, same_head) + _dot(lo, same_head)) * (1.0 / HEAD_DIM)
    oa_ref[0, :, :nqk] = (qk * lax.rsqrt(ms + NORM_EPS) * qkg_ref[...]).astype(BF16)
    oa_ref[0, :, nqk:] = pa[:, nqk:].astype(BF16)


def in_projection(x, sc, sh, g, w_bf16, q_gain, k_gain, widths, tile):
    b, s, d = x.shape
    cuts = tuple(int(sum(widths[:i + 1])) for i in range(4))
    row = lambda bi, i: (bi, i, 0)
    vec = lambda bi, i: (bi, 0, 0)
    dts = (BF16, F32, F32, BF16)
    qkg = jnp.concatenate([jnp.tile(q_gain.astype(F32), N_HEADS) * (HEAD_DIM ** -0.5),
                           jnp.tile(k_gain.astype(F32), N_HEADS)]).reshape(1, 2 * GROUP)
    return pl.pallas_call(
        functools.partial(_inproj_kernel, cuts=cuts),
        grid=(b, s // tile),
        in_specs=[
            pl.BlockSpec((1, tile, d), row),
            pl.BlockSpec((1, 1, d), vec),
            pl.BlockSpec((1, 1, d), vec),
            _const_spec((1, d)),
            _const_spec(w_bf16.shape),
            _const_spec((1, 2 * GROUP)),
        ],
        out_specs=[pl.BlockSpec((1, tile, w), row) for w in widths],
        out_shape=[jax.ShapeDtypeStruct((b, s, w), dt) for w, dt in zip(widths, dts)],
        compiler_params=_params("arbitrary", "arbitrary"),
        name="in_projection",
    )(x, sc, sh, g.reshape(1, d), w_bf16, qkg)


def _attn_kernel(*refs, nb):
    q_ref, k_refs, v_refs = refs[0], refs[1:1 + nb], refs[1 + nb:1 + 2 * nb]
    bias_ref, o_ref = refs[1 + 2 * nb:]
    heads = range(N_HEADS)
    lane = lax.broadcasted_iota(jnp.int32, (1, 2 * HEAD_DIM), 1)
    keep = [lane < HEAD_DIM, lane >= HEAD_DIM]
    groups = [slice(g * 2 * HEAD_DIM, (g + 1) * 2 * HEAD_DIM) for g in range(N_HEADS // 2)]
    q2 = [q_ref[0, :, g] for g in groups]
    k2 = [jnp.concatenate([r[0, :, g] for r in k_refs], axis=0) for g in groups]
    v2 = [jnp.concatenate([r[0, :, g] for r in v_refs], axis=0) for g in groups]
    qs = [jnp.where(keep[h % 2], q2[h // 2], jnp.zeros_like(q2[0])) for h in heads]
    ss = [_dot_nt(qs[h], k2[h // 2]) + bias_ref[0, h] for h in heads]
    ps = [jnp.exp(s - jnp.max(s, axis=-1, keepdims=True)) for s in ss]
    dens = [jnp.sum(p, axis=-1, keepdims=True) for p in ps]
    outs = [_dot(ps[h].astype(BF16), v2[h // 2]) / dens[h] for h in heads]
    for g, lanes in enumerate(groups):
        o_ref[0, :, lanes] = jnp.where(keep[0], outs[2 * g], outs[2 * g + 1])


def attention_bias(rel_bias, qb):
    nh = rel_bias.shape[0]
    nb = 1 + ATTN_LEFT_CHUNKS * CHUNK // qb
    nk = nb * qb
    back = nk - qb
    qi = jnp.arange(qb)[:, None]
    kj = jnp.arange(nk)[None, :]
    qc = (qi + back) // CHUNK
    kc = kj // CHUNK
    valid = jnp.logical_and(kc <= qc, kc >= qc - ATTN_LEFT_CHUNKS)
    m = jnp.arange(nk + 1)
    rel = jnp.where(m < back + CHUNK,
                    jnp.clip(back - m, -ATTN_MAX_REL, ATTN_MAX_REL) + ATTN_MAX_REL, 2 * ATTN_MAX_REL)
    vec = rel_bias.astype(F32)[:, rel]
    toeplitz = jnp.tile(vec, (1, qb))[:, :qb * nk].reshape(nh, qb, nk)
    first_key = (nb - 1 - jnp.arange(nb))[:, None, None, None] * qb
    live = jnp.logical_and(valid[None, None], kj[None, None] >= first_key)
    return jnp.where(live, toeplitz[None], NEG_BIG)


def chunk_attention(pa, bias, qb):
    b, s, _ = pa.shape
    nb = bias.shape[0]
    blk = (1, qb, GROUP)

    def window(col):
        return [pl.BlockSpec(blk, functools.partial(
            lambda bi, i, back: (bi, jnp.maximum(i - back, 0), col), back=back))
            for back in range(nb - 1, -1, -1)]

    return pl.pallas_call(
        functools.partial(_attn_kernel, nb=nb),
        grid=(b, s // qb),
        in_specs=[pl.BlockSpec(blk, lambda bi, i: (bi, i, 0))] + window(1) + window(2) + [
            pl.BlockSpec((1,) + bias.shape[1:], lambda bi, i: (jnp.minimum(i, nb - 1), 0, 0, 0)),
        ],
        out_specs=pl.BlockSpec(blk, lambda bi, i: (bi, i, 0)),
        out_shape=jax.ShapeDtypeStruct((b, s, GROUP), F32),
        compiler_params=_params("arbitrary", "arbitrary"),
        name="chunk_attention",
    )(*([pa] * (1 + 2 * nb)), bias)


def _lru_kernel(p_ref, cw_ref, cb_ref, raw_ref, rab_ref, riw_ref, rib_ref, lam_ref, o_ref,
                xpad_ref, h_ref):
    i = pl.program_id(1)
    t = p_ref.shape[1]

    @pl.when(i == 0)
    def _():
        xpad_ref[0:HALO, :] = jnp.zeros((HALO, GROUP), F32)
        h_ref[...] = jnp.zeros_like(h_ref)

    @pl.when(i > 0)
    def _():
        xpad_ref[0:HALO, :] = xpad_ref[t:t + HALO, :]

    xpad_ref[HALO:HALO + t, :] = p_ref[0, :, 0:GROUP]
    gate = p_ref[0, :, GROUP:2 * GROUP]
    xc = cb_ref[...]
    for j in range(LRU_CONV):
        xc = xc + cw_ref[j:j + 1, :] * xpad_ref[pl.ds(HALO - (LRU_CONV - 1) + j, t), :]
    xcb = xc.astype(BF16)
    r_gate = _sigmoid(_dot(xcb, raw_ref[...]) + rab_ref[...])
    i_gate = _sigmoid(_dot(xcb, riw_ref[...]) + rib_ref[...])
    log_a = (-LRU_C * r_gate) * _softplus(-lam_ref[...])
    a = jnp.exp(log_a)
    u = jnp.sqrt(-jnp.tanh(log_a) * (a * a + 1.0)) * (i_gate * xc)
    rows = lax.broadcasted_iota(jnp.int32, (t, 1), 0)
    k = 1
    while k < t:
        keep = rows >= k
        a_sh = jnp.where(keep, pltpu.roll(a, k, 0), 1.0)
        u_sh = jnp.where(keep, pltpu.roll(u, k, 0), 0.0)
        u = a * u_sh + u
        a = a * a_sh
        k *= 2
    hh = a * h_ref[...] + u
    h_ref[...] = hh[t - 1:t, :]
    o_ref[0] = hh * _gelu_tanh(gate)


def _block_diag(w):
    h, n, _ = w.shape
    eye = jnp.eye(h, dtype=w.dtype)
    return (eye[:, None, :, None] * w[:, :, None, :]).reshape(h * n, h * n)


def rglru(pb, conv_w, conv_b, ra_w, ra_b, ri_w, ri_b, lam, tile):
    b, s, _ = pb.shape
    row = lambda bi, i: (bi, i, 0)
    vec = lambda a: a.reshape(1, GROUP)
    return pl.pallas_call(
        _lru_kernel,
        grid=(b, s // tile),
        in_specs=[
            pl.BlockSpec((1, tile, 2 * GROUP), row),
            _const_spec((LRU_CONV, GROUP)),
            _const_spec((1, GROUP)),
            _const_spec((GROUP, GROUP)),
            _const_spec((1, GROUP)),
            _const_spec((GROUP, GROUP)),
            _const_spec((1, GROUP)),
            _const_spec((1, GROUP)),
        ],
        out_specs=pl.BlockSpec((1, tile, GROUP), row),
        out_shape=jax.ShapeDtypeStruct((b, s, GROUP), F32),
        scratch_shapes=[pltpu.VMEM((tile + HALO, GROUP), F32), pltpu.VMEM((1, GROUP), F32)],
        compiler_params=_params("arbitrary", "arbitrary"),
        name="rglru",
    )(pb, conv_w, vec(conv_b), _block_diag(ra_w).astype(BF16), vec(ra_b),
      _block_diag(ri_w).astype(BF16), vec(ri_b), vec(lam))


def _rwkv_kernel(p_ref, mu_ref, w0_ref, w2_ref, a0_ref, a2_ref, g2_ref, kk_ref, ka_ref, rk_ref,
                 lnw_ref, lnb_ref, o_ref, ppad_ref, state_ref, y_ref):
    i = pl.program_id(1)
    nbat, t, width = p_ref.shape
    ch = RWKV_CHUNK
    g3 = 3 * GROUP

    @pl.when(i == 0)
    def _():
        ppad_ref[:, 0:HALO, :] = jnp.zeros((nbat, HALO, width), F32)
        state_ref[...] = jnp.zeros_like(state_ref)

    @pl.when(i > 0)
    def _():
        ppad_ref[:, 0:HALO, :] = ppad_ref[:, t:t + HALO, :]

    ppad_ref[:, HALO:HALO + t, :] = p_ref[...]
    p = jnp.concatenate([p_ref[b] for b in range(nbat)], axis=0)
    p_prev = jnp.concatenate([ppad_ref[b, pl.ds(HALO - 1, t), :] for b in range(nbat)], axis=0)
    p = p + (p_prev - p) * mu_ref[...]
    r = p[:, 0:GROUP]
    k = p[:, GROUP:2 * GROUP]
    v = p[:, 2 * GROUP:g3]
    low = p[:, g3:width]
    w = -_softplus(-(w0_ref[...] + _dot(jnp.tanh(low).astype(BF16), w2_ref[...]))) - 0.5
    a = _sigmoid(a0_ref[...] + _dot(low.astype(BF16), a2_ref[...]))
    out_gate = _dot(_sigmoid(low).astype(BF16), g2_ref[...])

    hr = lax.broadcasted_iota(jnp.int32, (GROUP, GROUP), 0) // HEAD_DIM
    hc = lax.broadcasted_iota(jnp.int32, (GROUP, GROUP), 1) // HEAD_DIM
    head_ones = (hr == hc).astype(BF16)

    def split(x):
        hi = x.astype(BF16)
        return hi, (x - hi.astype(F32)).astype(BF16)

    def head_sum(x):
        hi, lo = split(x)
        return _dot(hi, head_ones) + _dot(lo, head_ones)

    kk = k * kk_ref[...]
    kk = kk / jnp.maximum(jnp.sqrt(head_sum(kk * kk)), 1e-12)
    k = k * (1.0 + (a - 1.0) * ka_ref[...])

    logd = -jnp.exp(w)
    kka = kk * a
    per_seq = t // ch
    nch = nbat * per_seq
    rows = [slice(c * ch, (c + 1) * ch) for c in range(nch)]
    pair = 2 * HEAD_DIM
    groups2 = [slice(g * pair, (g + 1) * pair) for g in range(N_HEADS // 2)]
    lane = lax.broadcasted_iota(jnp.int32, (1, pair), 1)
    first_head = lane < HEAD_DIM

    def blk(x, c, g):
        return x[rows[c], groups2[g]]

    def block_diag(x2):
        zero = jnp.zeros_like(x2)
        return jnp.concatenate([jnp.where(first_head, x2, zero), jnp.where(first_head, zero, x2)], axis=0)

    def diag_blocks(full):
        return jnp.where(first_head, full[:HEAD_DIM], full[HEAD_DIM:])

    rr = lax.broadcasted_iota(jnp.int32, (ch, ch), 0)
    cc = lax.broadcasted_iota(jnp.int32, (ch, ch), 1)
    tril = (rr >= cc).astype(BF16)
    er = lax.broadcasted_iota(jnp.int32, (ch, pair), 0)
    ec = jnp.bitwise_and(lax.broadcasted_iota(jnp.int32, (ch, pair), 1), ch - 1)
    eye2 = (er == ec).astype(F32)
    gr = lax.broadcasted_iota(jnp.int32, (2 * ch, 2 * pair), 0)
    gc = jnp.bitwise_and(lax.broadcasted_iota(jnp.int32, (2 * ch, 2 * pair), 1), ch - 1)
    gram_mask = jnp.logical_or(jnp.logical_and(gr < ch, gc < gr),
                               jnp.logical_and(gr >= ch, gc <= gr - ch))

    logd_hi, logd_lo = split(logd)
    cums = [_dot(tril, logd_hi[rs, :]) + _dot(tril, logd_lo[rs, :]) for rs in rows]
    cum = jnp.concatenate(cums, axis=0)
    cum_end = jnp.concatenate([jnp.broadcast_to(cm[ch - 1:ch, :], (ch, GROUP)) for cm in cums], axis=0)
    p_inv = jnp.exp(-cum)
    p_end = jnp.exp(cum_end - cum)
    a_t = (-kk * jnp.exp(cum - logd)).astype(BF16)
    b_t = (kka * p_inv).astype(BF16)
    k_t = (k * p_inv).astype(BF16)
    r_t = (r * jnp.exp(cum)).astype(BF16)
    b_h = (kka * p_end).astype(BF16)
    k_h = (k * p_end).astype(BF16)
    v_b = v.astype(BF16)
    decay_end = [jnp.exp(cm[ch - 1:ch, :]) for cm in cums]
    pairs = range(len(groups2))
    ready = {}

    def independent(chunks):
        prs = [(c, g) for c in chunks for g in pairs]
        grams = [jnp.where(gram_mask,
                           _dot_nt(jnp.concatenate([blk(a_t, c, g), blk(r_t, c, g)], axis=0),
                                   jnp.concatenate([block_diag(blk(b_t, c, g)), block_diag(blk(k_t, c, g))],
                                                   axis=0)), 0.0)
                 for c, g in prs]
        yield
        a_ak = [gm[:ch, pair:].astype(BF16) for gm in grams]
        g_bot = [gm[ch:].astype(BF16) for gm in grams]
        tinvs = [eye2 + gm[:ch, :pair] for gm in grams]
        npows = [gm[:ch, :pair].astype(BF16) for gm in grams]
        steps = 1
        while 2 * steps < ch:
            npows = [_dot(pw, block_diag(pw)).astype(BF16) for pw in npows]
            yield
            tinvs = [ti + _dot(ti.astype(BF16), block_diag(pw)) for ti, pw in zip(tinvs, npows)]
            yield
            steps *= 2
        ws = [_dot(ak, block_diag(blk(v_b, c, g))) for ak, (c, g) in zip(a_ak, prs)]
        gvs = [diag_blocks(_dot_tn(blk(v_b, c, g), blk(k_h, c, g))) for c, g in prs]
        for j, pr in enumerate(prs):
            ready[pr] = (g_bot[j], tinvs[j].astype(BF16), ws[j], gvs[j])

    states = [[state_ref[b, g] for g in pairs] for b in range(nbat)]

    def dependent(chunks):
        sts = states[chunks[0] // per_seq]
        for c in chunks:
            g_bot, tinv, w, gv = zip(*[ready[(c, g)] for g in pairs])
            stb = [block_diag(st.astype(BF16)) for st in sts]
            rhs = [_dot_nt(blk(a_t, c, g), stb[g]) + w[g] for g in pairs]
            yield
            us = [_dot(tinv[g], block_diag(rhs[g].astype(BF16))).astype(BF16) for g in pairs]
            yield
            ys = [_dot_nt(blk(r_t, c, g), stb[g])
                  + _dot(g_bot[g], jnp.concatenate([block_diag(us[g]), block_diag(blk(v_b, c, g))], axis=0))
                  for g in pairs]
            for g in pairs:
                sts[g] = (sts[g] * decay_end[c][:, groups2[g]]
                          + diag_blocks(_dot_tn(us[g], blk(b_h, c, g))) + gv[g])
                y_ref[rows[c], groups2[g]] = ys[g]
            yield

    waves = [[list(range(b * per_seq + c, b * per_seq + min(c + RWKV_GROUP, per_seq))) for b in range(nbat)]
             for c in range(0, per_seq, RWKV_GROUP)]
    _interleave(*[independent(grp) for grp in waves[0]])
    for done, coming in zip(waves, waves[1:]):
        _interleave(*([dependent(grp) for grp in done] + [independent(grp) for grp in coming]))
    _interleave(*[dependent(grp) for grp in waves[-1]])
    for b in range(nbat):
        for g in pairs:
            state_ref[b, g] = states[b][g]

    y = y_ref[...]
    mean = head_sum(y) * (1.0 / HEAD_DIM)
    yc = y - mean
    var = head_sum(yc * yc) * (1.0 / HEAD_DIM)
    y = yc * lax.rsqrt(var + RWKV_LN_EPS) * lnw_ref[...] + lnb_ref[...]
    y = y + head_sum(r * k * rk_ref[...]) * v
    y = y * out_gate
    for b in range(nbat):
        o_ref[b] = y[b * t:(b + 1) * t]


def _pad_rows(w, start, total):
    return jnp.zeros((total, w.shape[1]), w.dtype).at[start:start + w.shape[0]].set(w)


def rwkv7(pc, mu, w0, w2, a0, a2, g2, k_k, k_a, r_k, ln_w, ln_b, tile):
    b, s, width = pc.shape
    assert RWKV_CHUNK == HEAD_DIM, "the side-by-side head layout needs square per-head chunk matrices"
    low = width - 3 * GROUP
    dr, ir = w2.shape[0], a2.shape[0]
    nbat = RWKV_SEQS if b % RWKV_SEQS == 0 else 1
    row = lambda bi, i: (bi, i, 0)
    vec = lambda a: a.reshape(1, -1)
    return pl.pallas_call(
        _rwkv_kernel,
        grid=(b // nbat, s // tile),
        in_specs=[
            pl.BlockSpec((nbat, tile, width), row),
            _const_spec((1, width)),
            _const_spec((1, GROUP)),
            _const_spec((low, GROUP)),
            _const_spec((1, GROUP)),
            _const_spec((low, GROUP)),
            _const_spec((low, GROUP)),
            _const_spec((1, GROUP)),
            _const_spec((1, GROUP)),
            _const_spec((1, GROUP)),
            _const_spec((1, GROUP)),
            _const_spec((1, GROUP)),
        ],
        out_specs=pl.BlockSpec((nbat, tile, GROUP), row),
        out_shape=jax.ShapeDtypeStruct((b, s, GROUP), F32),
        scratch_shapes=[
            pltpu.VMEM((nbat, tile + HALO, width), F32),
            pltpu.VMEM((nbat, N_HEADS // 2, HEAD_DIM, 2 * HEAD_DIM), F32),
            pltpu.VMEM((nbat * tile, GROUP), F32),
        ],
        compiler_params=_params("arbitrary", "arbitrary"),
        name="rwkv7",
    )(pc, vec(mu), vec(w0), _pad_rows(w2, 0, low).astype(BF16), vec(a0),
      _pad_rows(a2, dr, low).astype(BF16), _pad_rows(g2, dr + ir, low).astype(BF16),
      vec(k_k), vec(k_a), vec(r_k), vec(ln_w), vec(ln_b))


def _sb_kernel(q_ref, k_ref, v_ref, o_ref):
    qi = pl.program_id(1)
    blk = q_ref.shape[1]
    rr = lax.broadcasted_iota(jnp.int32, (blk, 2 * blk), 0)
    cc = lax.broadcasted_iota(jnp.int32, (blk, 2 * blk), 1)
    sums = jnp.logical_or(rr >= cc, cc >= blk).astype(BF16)
    causal = (lax.broadcasted_iota(jnp.int32, (blk, blk), 1) < lax.broadcasted_iota(jnp.int32, (blk, blk), 0))
    scale = HEAD_DIM ** -0.5
    qs = [q_ref[0, :, h * HEAD_DIM:(h + 1) * HEAD_DIM] * scale for h in range(N_HEADS)]

    heads = range(N_HEADS)
    lanes = [slice(h * HEAD_DIM, (h + 1) * HEAD_DIM) for h in heads]

    def split_sums(lk):
        hi = lk.astype(BF16)
        lo = (lk - hi.astype(F32)).astype(BF16)
        return _dot(hi, sums) + _dot(lo, sums)

    def block_pass(first_kb, count, diagonal, runs):
        kbs = [first_kb - n for n in range(count)]
        exists = [None if n == 0 else kb >= 0 for n, kb in enumerate(kbs)]
        rows = [pl.ds(pl.multiple_of(jnp.maximum(kb, 0) * blk, blk), blk) for kb in kbs]
        zs = [[_dot_nt(qs[h], k_ref[0, rw, lanes[h]]) for h in heads] for rw in rows]
        lks = [[-_softplus(z) for z in zb] for zb in zs]
        if diagonal:
            lks[0] = [jnp.where(causal, lk, 0.0) for lk in lks[0]]
        for n in range(1, count):
            lks[n] = [jnp.where(exists[n], lk, 0.0) for lk in lks[n]]
        ss = [[split_sums(lk) for lk in lb] for lb in lks]
        logw = []
        for n in range(count):
            if n == 0 and diagonal:
                logw.append([jnp.where(causal, z + s[:, :blk], NEG_BIG) for z, s in zip(zs[0], ss[0])])
                runs = [s[:, blk:] for s in ss[0]]
                continue
            lw = [z + s[:, :blk] + run for z, s, run in zip(zs[n], ss[n], runs)]
            if n > 0:
                lw = [jnp.where(exists[n], x, NEG_BIG) for x in lw]
            logw.append(lw)
            runs = [run + s[:, blk:] for s, run in zip(ss[n], runs)]
        pvs = [functools.reduce(lambda a, b: a + b,
                                [_dot(jnp.exp(logw[n][h]).astype(BF16), v_ref[0, rows[n], lanes[h]])
                                 for n in range(count)]) for h in heads]
        return pvs, runs

    accs, runs = block_pass(qi, SB_FIRST_BLOCKS, True, None)

    def alive(runs):
        top = functools.reduce(jnp.maximum, runs)[:, 0:1]
        return jnp.max(top, axis=0, keepdims=True)[0, 0] > SB_EXIT

    def cond(carry):
        j, live, _, _ = carry
        return jnp.logical_and(qi - SB_FIRST_BLOCKS - SB_LOOP_BLOCKS * j >= 0, live)

    def body(carry):
        j, _, accs, runs = carry
        pvs, runs = block_pass(qi - SB_FIRST_BLOCKS - SB_LOOP_BLOCKS * j, SB_LOOP_BLOCKS, False, runs)
        return j + 1, alive(runs), tuple(a + p for a, p in zip(accs, pvs)), tuple(runs)

    _, _, accs, _ = lax.while_loop(cond, body, (0, alive(runs), tuple(accs), tuple(runs)))
    for h in heads:
        o_ref[0, :, lanes[h]] = accs[h]


def stick_breaking(pd_bf16, blk):
    b, s, _ = pd_bf16.shape
    return pl.pallas_call(
        _sb_kernel,
        grid=(b, s // blk),
        in_specs=[
            pl.BlockSpec((1, blk, GROUP), lambda bi, i: (bi, i, 0)),
            pl.BlockSpec((1, s, GROUP), lambda bi, i: (bi, 0, 1)),
            pl.BlockSpec((1, s, GROUP), lambda bi, i: (bi, 0, 2)),
        ],
        out_specs=pl.BlockSpec((1, blk, GROUP), lambda bi, i: (bi, i, 0)),
        out_shape=jax.ShapeDtypeStruct((b, s, GROUP), F32),
        compiler_params=_params("arbitrary", "arbitrary"),
        name="stick_breaking",
    )(pd_bf16, pd_bf16, pd_bf16)


def _outproj_kernel(x_ref, ya_ref, yb_ref, yc_ref, yd_ref, gt_ref, w_ref, o_ref):
    y = jnp.concatenate([ya_ref[0], yb_ref[0], yc_ref[0], yd_ref[0]], axis=-1).astype(BF16)
    o_ref[0] = x_ref[0] + gt_ref[0] * _dot(y, w_ref[...])


def out_projection(x, ys, gt, w_bf16, tile):
    b, s, d = x.shape
    row = lambda bi, i: (bi, i, 0)
    return pl.pallas_call(
        _outproj_kernel,
        grid=(b, s // tile),
        in_specs=[pl.BlockSpec((1, tile, d), row)]
        + [pl.BlockSpec((1, tile, GROUP), row) for _ in ys]
        + [pl.BlockSpec((1, 1, d), lambda bi, i: (bi, 0, 0)), _const_spec(w_bf16.shape)],
        out_specs=pl.BlockSpec((1, tile, d), row),
        out_shape=jax.ShapeDtypeStruct((b, s, d), F32),
        compiler_params=_params("arbitrary", "arbitrary"),
        name="out_projection",
    )(x, *ys, gt, w_bf16)


def _ffn_kernel(x_ref, xh_ref, sc_ref, sh_ref, gt_ref, g_ref, wv_ref, wg_ref, cwv_ref, cwg_ref,
                cbv_ref, cbg_ref, wd_ref, o_ref, uv_ref, ug_ref, act_ref):
    i = pl.program_id(1)
    t = x_ref.shape[1]
    x = x_ref[0]
    xe = jnp.concatenate([xh_ref[0], x], axis=0)
    h = _modulated_norm(xe, g_ref[...], sc_ref[0], sh_ref[0]).astype(BF16)
    first = i == 0

    def up(f):
        for u_ref, w_ref in ((uv_ref, wv_ref), (ug_ref, wg_ref)):
            u = _dot(h, w_ref[f])
            u_ref[f % 2, 0:HALO, :] = jnp.where(first, 0.0, u[0:HALO])
            u_ref[f % 2, HALO:, :] = u[HALO:]

    def conv(u_ref, f, cw, cb):
        out = cb
        for j in range(FFN_CONV):
            out = out + cw[j:j + 1, :] * u_ref[f % 2, pl.ds(HALO - (FFN_CONV - 1) + j, t), :]
        return out

    nf = wv_ref.shape[0]
    fc = wv_ref.shape[2]
    up(0)
    for f in range(nf):
        if f + 1 < nf:
            up(f + 1)
        val = conv(uv_ref, f, cwv_ref[f], cbv_ref[f])
        gate = conv(ug_ref, f, cwg_ref[f], cbg_ref[f])
        act_ref[:, f * fc:(f + 1) * fc] = (val * _gelu_tanh(gate)).astype(BF16)
    o_ref[0] = x + gt_ref[0] * _dot(act_ref[...], wd_ref[...])


def conv_ffn(x, sc, sh, gt, g, w_up, conv_w, conv_b, w_down, tile, fchunk):
    b, s, d = x.shape
    f = w_down.shape[0]
    nf = f // fchunk
    row = lambda bi, i: (bi, i, 0)
    vec = lambda bi, i: (bi, 0, 0)
    halo = lambda bi, i: (bi, jnp.maximum(i * (tile // HALO) - 1, 0), 0)
    cols = lambda w: w.reshape(w.shape[0], nf, fchunk).transpose(1, 0, 2)
    wv = cols(w_up[:, :f]).astype(BF16)
    wg = cols(w_up[:, f:]).astype(BF16)
    wd = w_down.astype(BF16)
    cwv, cwg = cols(conv_w[:, :f]), cols(conv_w[:, f:])
    cbv, cbg = cols(conv_b[None, :f]), cols(conv_b[None, f:])
    return pl.pallas_call(
        _ffn_kernel,
        grid=(b, s // tile),
        in_specs=[
            pl.BlockSpec((1, tile, d), row),
            pl.BlockSpec((1, HALO, d), halo),
            pl.BlockSpec((1, 1, d), vec),
            pl.BlockSpec((1, 1, d), vec),
            pl.BlockSpec((1, 1, d), vec),
            _const_spec((1, d)),
            _const_spec(wv.shape),
            _const_spec(wg.shape),
            _const_spec(cwv.shape),
            _const_spec(cwg.shape),
            _const_spec(cbv.shape),
            _const_spec(cbg.shape),
            _const_spec(wd.shape),
        ],
        out_specs=pl.BlockSpec((1, tile, d), row),
        out_shape=jax.ShapeDtypeStruct((b, s, d), F32),
        scratch_shapes=[
            pltpu.VMEM((2, tile + HALO, fchunk), F32),
            pltpu.VMEM((2, tile + HALO, fchunk), F32),
            pltpu.VMEM((tile, f), BF16),
        ],
        compiler_params=_params("arbitrary", "arbitrary"),
        name="conv_ffn",
    )(x, x, sc, sh, gt, g.reshape(1, d), wv, wg, cwv, cwg, cbv, cbg, wd)


def kernel(x, c, ada_w, ada_b, norm1_g, norm2_g, w_in, w_out, attn_q_gain, attn_k_gain, attn_rel_bias, lru_conv_w, lru_conv_b, lru_ra_w, lru_ra_b, lru_ri_w, lru_ri_b, lru_lambda, rwkv_mu, rwkv_w0, rwkv_w2, rwkv_a0, rwkv_a2, rwkv_g2, rwkv_k_k, rwkv_k_a, rwkv_r_k, rwkv_ln_w, rwkv_ln_b, ffn_up, ffn_conv_w, ffn_conv_b, ffn_down):
    depth = ada_w.shape[0]
    bsz, seq, d = x.shape
    p_rwkv = rwkv_mu.shape[1]
    widths = (3 * GROUP, 2 * GROUP, p_rwkv, 3 * GROUP)
    tile = min(SEQ_TILE, seq)
    mtile = min(MATMUL_TILE, seq)
    qb = min(ATTN_QB, seq)
    mods = ada_modulation(c, ada_w, ada_b)
    for l in range(depth):
        sh_m, sc_m, gt_m, sh_f, sc_f, gt_f = [
            mods[l, :, j * d:(j + 1) * d].reshape(bsz, 1, d) for j in range(6)]
        pa, pb, pc, pd = in_projection(x, sc_m, sh_m, norm1_g[l], w_in[l].astype(BF16),
                                       attn_q_gain[l], attn_k_gain[l], widths, mtile)
        y_a = chunk_attention(pa, attention_bias(attn_rel_bias[l], qb), qb)
        y_b = rglru(pb, lru_conv_w[l], lru_conv_b[l], lru_ra_w[l], lru_ra_b[l],
                    lru_ri_w[l], lru_ri_b[l], lru_lambda[l], tile)
        y_c = rwkv7(pc, rwkv_mu[l], rwkv_w0[l], rwkv_w2[l], rwkv_a0[l], rwkv_a2[l], rwkv_g2[l],
                    rwkv_k_k[l], rwkv_k_a[l], rwkv_r_k[l], rwkv_ln_w[l], rwkv_ln_b[l], tile)
        y_d = stick_breaking(pd, min(SB_BLOCK, seq))
        x = out_projection(x, (y_a, y_b, y_c, y_d), gt_m, w_out[l].astype(BF16), mtile)
        x = conv_ffn(x, sc_f, sh_f, gt_f, norm2_g[l], ffn_up[l], ffn_conv_w[l], ffn_conv_b[l],
                     ffn_down[l], mtile, FFN_CHUNK)
    return x
```

```python
import functools

import jax
import jax.numpy as jnp
from jax import lax
from jax.experimental import pallas as pl
from jax.experimental.pallas import tpu as pltpu

F32 = jnp.float32
BF16 = jnp.bfloat16
MIX_DTYPE = BF16
HI = lax.Precision.HIGHEST

HEAD_DIM = 64
N_HEADS = 4
GROUP = HEAD_DIM * N_HEADS
CHUNK = 64
ATTN_LEFT_CHUNKS = 8
ATTN_MAX_REL = 128
LRU_CONV = 4
LRU_C = 8.0
RWKV_LN_EPS = 64e-5
FFN_CONV = 3
NORM_EPS = 1e-6
HALO = 8
NEG_BIG = -1e30
SB_EXIT = -104.0

VMEM_LIMIT = 56 * 1024 * 1024

ATTN_QB = 256
SEQ_TILE = 512
MATMUL_TILE = 1024
SB_BLOCK = 128
SB_FIRST_BLOCKS = 3
SB_LOOP_BLOCKS = 2
RWKV_CHUNK = 64
RWKV_GROUP = 4
RWKV_SEQS = 4
FFN_CHUNK = 256


def _params(*sem):
    return pltpu.CompilerParams(dimension_semantics=sem, vmem_limit_bytes=VMEM_LIMIT)


def _sigmoid(x):
    return 1.0 / (1.0 + jnp.exp(-x))


def _softplus(x):
    return jnp.maximum(x, 0.0) + jnp.log(1.0 + jnp.exp(-jnp.abs(x)))


def _gelu_tanh(x):
    return 0.5 * x * (1.0 + jnp.tanh(0.7978845608028654 * (x + 0.044715 * x * x * x)))


def _dot_nt(a, b, **kw):
    return lax.dot_general(a, b, (((1,), (1,)), ((), ())), preferred_element_type=F32, **kw)


def _dot_tn(a, b, **kw):
    return lax.dot_general(a, b, (((0,), (0,)), ((), ())), preferred_element_type=F32, **kw)


def _dot(a, b, **kw):
    return jnp.dot(a, b, preferred_element_type=F32, **kw)


def _interleave(*stage_generators):
    live = list(stage_generators)
    done = object()
    while live:
        for gen in list(live):
            if next(gen, done) is done:
                live.remove(gen)


def _const_spec(shape):
    nd = len(shape)
    return pl.BlockSpec(shape, lambda *_: (0,) * nd, pipeline_mode=pl.Buffered(1))


def _ada_kernel(c_ref, w_ref, b_ref, o_ref):
    c = c_ref[...]
    s = c * _sigmoid(c)
    o_ref[0] = _dot(s, w_ref[0], precision=HI) + b_ref[0]


def ada_modulation(c, ada_w, ada_b):
    nl, d, n = ada_w.shape
    b = c.shape[0]
    rows = 8
    cp = jnp.zeros((rows, d), F32).at[:b].set(c)
    tn = 1536
    out = pl.pallas_call(
        _ada_kernel,
        grid=(nl, n // tn),
        in_specs=[
            pl.BlockSpec((rows, d), lambda l, j: (0, 0)),
            pl.BlockSpec((1, d, tn), lambda l, j: (l, 0, j)),
            pl.BlockSpec((1, 1, tn), lambda l, j: (l, 0, j)),
        ],
        out_specs=pl.BlockSpec((1, rows, tn), lambda l, j: (l, 0, j)),
        out_shape=jax.ShapeDtypeStruct((nl, rows, n), F32),
        compiler_params=_params("arbitrary", "arbitrary"),
        name="ada_modulation",
    )(cp, ada_w, ada_b.reshape(nl, 1, n))
    return out[:, :b]


def _modulated_norm(x, g, sc, sh):
    ms = jnp.mean(x * x, axis=-1, keepdims=True)
    return (x * lax.rsqrt(ms + NORM_EPS) * g) * (1.0 + sc) + sh


def _inproj_kernel(x_ref, sc_ref, sh_ref, g_ref, w_ref, qkg_ref, oa_ref, ob_ref, oc_ref, od_ref, *, cuts):
    h = _modulated_norm(x_ref[0], g_ref[...], sc_ref[0], sh_ref[0]).astype(BF16)
    c0, c1, c2, c3 = cuts
    pa = _dot(h, w_ref[:, 0:c0])
    ob_ref[0] = _dot(h, w_ref[:, c0:c1])
    oc_ref[0] = _dot(h, w_ref[:, c1:c2])
    od_ref[0] = _dot(h, w_ref[:, c2:c3]).astype(BF16)
    nqk = 2 * GROUP
    qk = pa[:, :nqk]
    sq = qk * qk
    hi = sq.astype(BF16)
    lo = (sq - hi.astype(F32)).astype(BF16)
    hr = lax.broadcasted_iota(jnp.int32, (nqk, nqk), 0) // HEAD_DIM
    hc = lax.broadcasted_iota(jnp.int32, (nqk, nqk), 1) // HEAD_DIM
    same_head = (hr == hc).astype(BF16)
    ms = (_dot(hi, same_head) + _dot(lo, same_head)) * (1.0 / HEAD_DIM)
    oa_ref[0, :, :nqk] = (qk * lax.rsqrt(ms + NORM_EPS) * qkg_ref[...]).astype(BF16)
    oa_ref[0, :, nqk:] = pa[:, nqk:].astype(BF16)


def in_projection(x, sc, sh, g, w_bf16, q_gain, k_gain, widths, tile):
    b, s, d = x.shape
    cuts = tuple(int(sum(widths[:i + 1])) for i in range(4))
    row = lambda bi, i: (bi, i, 0)
    vec = lambda bi, i: (bi, 0, 0)
    dts = (BF16, F32, F32, BF16)
    qkg = jnp.concatenate([jnp.tile(q_gain.astype(F32), N_HEADS) * (HEAD_DIM ** -0.5),
                           jnp.tile(k_gain.astype(F32), N_HEADS)]).reshape(1, 2 * GROUP)
    return pl.pallas_call(
        functools.partial(_inproj_kernel, cuts=cuts),
        grid=(b, s // tile),
        in_specs=[
            pl.BlockSpec((1, tile, d), row),
            pl.BlockSpec((1, 1, d), vec),
            pl.BlockSpec((1, 1, d), vec),
            _const_spec((1, d)),
            _const_spec(w_bf16.shape),
            _const_spec((1, 2 * GROUP)),
        ],
        out_specs=[pl.BlockSpec((1, tile, w), row) for w in widths],
        out_shape=[jax.ShapeDtypeStruct((b, s, w), dt) for w, dt in zip(widths, dts)],
        compiler_params=_params("arbitrary", "arbitrary"),
        name="in_projection",
    )(x, sc, sh, g.reshape(1, d), w_bf16, qkg)


def _attn_kernel(*refs, nb):
    q_ref, k_refs, v_refs = refs[0], refs[1:1 + nb], refs[1 + nb:1 + 2 * nb]
    bias_ref, o_ref = refs[1 + 2 * nb:]
    heads = range(N_HEADS)
    lane = lax.broadcasted_iota(jnp.int32, (1, 2 * HEAD_DIM), 1)
    keep = [lane < HEAD_DIM, lane >= HEAD_DIM]
    groups = [slice(g * 2 * HEAD_DIM, (g + 1) * 2 * HEAD_DIM) for g in range(N_HEADS // 2)]
    q2 = [q_ref[0, :, g] for g in groups]
    k2 = [jnp.concatenate([r[0, :, g] for r in k_refs], axis=0) for g in groups]
    v2 = [jnp.concatenate([r[0, :, g] for r in v_refs], axis=0) for g in groups]
    qs = [jnp.where(keep[h % 2], q2[h // 2], jnp.zeros_like(q2[0])) for h in heads]
    ss = [_dot_nt(qs[h], k2[h // 2]) + bias_ref[0, h] for h in heads]
    ps = [jnp.exp(s - jnp.max(s, axis=-1, keepdims=True)) for s in ss]
    dens = [jnp.sum(p, axis=-1, keepdims=True) for p in ps]
    outs = [_dot(ps[h].astype(BF16), v2[h // 2]) / dens[h] for h in heads]
    for g, lanes in enumerate(groups):
        o_ref[0, :, lanes] = jnp.where(keep[0], outs[2 * g], outs[2 * g + 1]).astype(o_ref.dtype)


def attention_bias(rel_bias, qb):
    nh = rel_bias.shape[0]
    nb = 1 + ATTN_LEFT_CHUNKS * CHUNK // qb
    nk = nb * qb
    back = nk - qb
    qi = jnp.arange(qb)[:, None]
    kj = jnp.arange(nk)[None, :]
    qc = (qi + back) // CHUNK
    kc = kj // CHUNK
    valid = jnp.logical_and(kc <= qc, kc >= qc - ATTN_LEFT_CHUNKS)
    m = jnp.arange(nk + 1)
    rel = jnp.where(m < back + CHUNK,
                    jnp.clip(back - m, -ATTN_MAX_REL, ATTN_MAX_REL) + ATTN_MAX_REL, 2 * ATTN_MAX_REL)
    vec = rel_bias.astype(F32)[:, rel]
    toeplitz = jnp.tile(vec, (1, qb))[:, :qb * nk].reshape(nh, qb, nk)
    first_key = (nb - 1 - jnp.arange(nb))[:, None, None, None] * qb
    live = jnp.logical_and(valid[None, None], kj[None, None] >= first_key)
    return jnp.where(live, toeplitz[None], NEG_BIG)


def chunk_attention(pa, bias, qb):
    b, s, _ = pa.shape
    nb = bias.shape[0]
    blk = (1, qb, GROUP)

    def window(col):
        return [pl.BlockSpec(blk, functools.partial(
            lambda bi, i, back: (bi, jnp.maximum(i - back, 0), col), back=back))
            for back in range(nb - 1, -1, -1)]

    return pl.pallas_call(
        functools.partial(_attn_kernel, nb=nb),
        grid=(b, s // qb),
        in_specs=[pl.BlockSpec(blk, lambda bi, i: (bi, i, 0))] + window(1) + window(2) + [
            pl.BlockSpec((1,) + bias.shape[1:], lambda bi, i: (jnp.minimum(i, nb - 1), 0, 0, 0)),
        ],
        out_specs=pl.BlockSpec(blk, lambda bi, i: (bi, i, 0)),
        out_shape=jax.ShapeDtypeStruct((b, s, GROUP), MIX_DTYPE),
        compiler_params=_params("arbitrary", "arbitrary"),
        name="chunk_attention",
    )(*([pa] * (1 + 2 * nb)), bias)


def _lru_kernel(p_ref, cw_ref, cb_ref, raw_ref, rab_ref, riw_ref, rib_ref, lam_ref, o_ref,
                xpad_ref, h_ref):
    i = pl.program_id(1)
    t = p_ref.shape[1]

    @pl.when(i == 0)
    def _():
        xpad_ref[0:HALO, :] = jnp.zeros((HALO, GROUP), F32)
        h_ref[...] = jnp.zeros_like(h_ref)

    @pl.when(i > 0)
    def _():
        xpad_ref[0:HALO, :] = xpad_ref[t:t + HALO, :]

    xpad_ref[HALO:HALO + t, :] = p_ref[0, :, 0:GROUP]
    gate = p_ref[0, :, GROUP:2 * GROUP]
    xc = cb_ref[...]
    for j in range(LRU_CONV):
        xc = xc + cw_ref[j:j + 1, :] * xpad_ref[pl.ds(HALO - (LRU_CONV - 1) + j, t), :]
    xcb = xc.astype(BF16)
    r_gate = _sigmoid(_dot(xcb, raw_ref[...]) + rab_ref[...])
    i_gate = _sigmoid(_dot(xcb, riw_ref[...]) + rib_ref[...])
    log_a = (-LRU_C * r_gate) * _softplus(-lam_ref[...])
    a = jnp.exp(log_a)
    u = jnp.sqrt(-jnp.tanh(log_a) * (a * a + 1.0)) * (i_gate * xc)
    rows = lax.broadcasted_iota(jnp.int32, (t, 1), 0)
    k = 1
    while k < t:
        keep = rows >= k
        a_sh = jnp.where(keep, pltpu.roll(a, k, 0), 1.0)
        u_sh = jnp.where(keep, pltpu.roll(u, k, 0), 0.0)
        u = a * u_sh + u
        a = a * a_sh
        k *= 2
    hh = a * h_ref[...] + u
    h_ref[...] = hh[t - 1:t, :]
    o_ref[0] = (hh * _gelu_tanh(gate)).astype(o_ref.dtype)


def _block_diag(w):
    h, n, _ = w.shape
    eye = jnp.eye(h, dtype=w.dtype)
    return (eye[:, None, :, None] * w[:, :, None, :]).reshape(h * n, h * n)


def rglru(pb, conv_w, conv_b, ra_w, ra_b, ri_w, ri_b, lam, tile):
    b, s, _ = pb.shape
    row = lambda bi, i: (bi, i, 0)
    vec = lambda a: a.reshape(1, GROUP)
    return pl.pallas_call(
        _lru_kernel,
        grid=(b, s // tile),
        in_specs=[
            pl.BlockSpec((1, tile, 2 * GROUP), row),
            _const_spec((LRU_CONV, GROUP)),
            _const_spec((1, GROUP)),
            _const_spec((GROUP, GROUP)),
            _const_spec((1, GROUP)),
            _const_spec((GROUP, GROUP)),
            _const_spec((1, GROUP)),
            _const_spec((1, GROUP)),
        ],
        out_specs=pl.BlockSpec((1, tile, GROUP), row),
        out_shape=jax.ShapeDtypeStruct((b, s, GROUP), MIX_DTYPE),
        scratch_shapes=[pltpu.VMEM((tile + HALO, GROUP), F32), pltpu.VMEM((1, GROUP), F32)],
        compiler_params=_params("arbitrary", "arbitrary"),
        name="rglru",
    )(pb, conv_w, vec(conv_b), _block_diag(ra_w).astype(BF16), vec(ra_b),
      _block_diag(ri_w).astype(BF16), vec(ri_b), vec(lam))


def _rwkv_kernel(p_ref, mu_ref, w0_ref, w2_ref, a0_ref, a2_ref, g2_ref, kk_ref, ka_ref, rk_ref,
                 lnw_ref, lnb_ref, o_ref, ppad_ref, state_ref, y_ref):
    i = pl.program_id(1)
    nbat, t, width = p_ref.shape
    ch = RWKV_CHUNK
    g3 = 3 * GROUP

    @pl.when(i == 0)
    def _():
        ppad_ref[:, 0:HALO, :] = jnp.zeros((nbat, HALO, width), F32)
        state_ref[...] = jnp.zeros_like(state_ref)

    @pl.when(i > 0)
    def _():
        ppad_ref[:, 0:HALO, :] = ppad_ref[:, t:t + HALO, :]

    ppad_ref[:, HALO:HALO + t, :] = p_ref[...]
    p = jnp.concatenate([p_ref[b] for b in range(nbat)], axis=0)
    p_prev = jnp.concatenate([ppad_ref[b, pl.ds(HALO - 1, t), :] for b in range(nbat)], axis=0)
    p = p + (p_prev - p) * mu_ref[...]
    r = p[:, 0:GROUP]
    k = p[:, GROUP:2 * GROUP]
    v = p[:, 2 * GROUP:g3]
    low = p[:, g3:width]
    w = -_softplus(-(w0_ref[...] + _dot(jnp.tanh(low).astype(BF16), w2_ref[...]))) - 0.5
    a = _sigmoid(a0_ref[...] + _dot(low.astype(BF16), a2_ref[...]))
    out_gate = _dot(_sigmoid(low).astype(BF16), g2_ref[...])

    hr = lax.broadcasted_iota(jnp.int32, (GROUP, GROUP), 0) // HEAD_DIM
    hc = lax.broadcasted_iota(jnp.int32, (GROUP, GROUP), 1) // HEAD_DIM
    head_ones = (hr == hc).astype(BF16)

    def split(x):
        hi = x.astype(BF16)
        return hi, (x - hi.astype(F32)).astype(BF16)

    def head_sum(x):
        hi, lo = split(x)
        return _dot(hi, head_ones) + _dot(lo, head_ones)

    kk = k * kk_ref[...]
    kk = kk / jnp.maximum(jnp.sqrt(head_sum(kk * kk)), 1e-12)
    k = k * (1.0 + (a - 1.0) * ka_ref[...])

    logd = -jnp.exp(w)
    kka = kk * a
    per_seq = t // ch
    nch = nbat * per_seq
    rows = [slice(c * ch, (c + 1) * ch) for c in range(nch)]
    pair = 2 * HEAD_DIM
    groups2 = [slice(g * pair, (g + 1) * pair) for g in range(N_HEADS // 2)]
    lane = lax.broadcasted_iota(jnp.int32, (1, pair), 1)
    first_head = lane < HEAD_DIM

    def blk(x, c, g):
        return x[rows[c], groups2[g]]

    def block_diag(x2):
        zero = jnp.zeros_like(x2)
        return jnp.concatenate([jnp.where(first_head, x2, zero), jnp.where(first_head, zero, x2)], axis=0)

    def diag_blocks(full):
        return jnp.where(first_head, full[:HEAD_DIM], full[HEAD_DIM:])

    rr = lax.broadcasted_iota(jnp.int32, (ch, ch), 0)
    cc = lax.broadcasted_iota(jnp.int32, (ch, ch), 1)
    tril = (rr >= cc).astype(BF16)
    er = lax.broadcasted_iota(jnp.int32, (ch, pair), 0)
    ec = jnp.bitwise_and(lax.broadcasted_iota(jnp.int32, (ch, pair), 1), ch - 1)
    eye2 = (er == ec).astype(F32)
    gr = lax.broadcasted_iota(jnp.int32, (2 * ch, 2 * pair), 0)
    gc = jnp.bitwise_and(lax.broadcasted_iota(jnp.int32, (2 * ch, 2 * pair), 1), ch - 1)
    gram_mask = jnp.logical_or(jnp.logical_and(gr < ch, gc < gr),
                               jnp.logical_and(gr >= ch, gc <= gr - ch))

    logd_hi, logd_lo = split(logd)
    cums = [_dot(tril, logd_hi[rs, :]) + _dot(tril, logd_lo[rs, :]) for rs in rows]
    cum = jnp.concatenate(cums, axis=0)
    cum_end = jnp.concatenate([jnp.broadcast_to(cm[ch - 1:ch, :], (ch, GROUP)) for cm in cums], axis=0)
    p_inv = jnp.exp(-cum)
    p_end = jnp.exp(cum_end - cum)
    a_t = (-kk * jnp.exp(cum - logd)).astype(BF16)
    b_t = (kka * p_inv).astype(BF16)
    k_t = (k * p_inv).astype(BF16)
    r_t = (r * jnp.exp(cum)).astype(BF16)
    b_h = (kka * p_end).astype(BF16)
    k_h = (k * p_end).astype(BF16)
    v_b = v.astype(BF16)
    decay_end = [jnp.exp(cm[ch - 1:ch, :]) for cm in cums]
    pairs = range(len(groups2))
    ready = {}

    def independent(chunks):
        prs = [(c, g) for c in chunks for g in pairs]
        grams = [jnp.where(gram_mask,
                           _dot_nt(jnp.concatenate([blk(a_t, c, g), blk(r_t, c, g)], axis=0),
                                   jnp.concatenate([block_diag(blk(b_t, c, g)), block_diag(blk(k_t, c, g))],
                                                   axis=0)), 0.0)
                 for c, g in prs]
        yield
        a_ak = [gm[:ch, pair:].astype(BF16) for gm in grams]
        g_bot = [gm[ch:].astype(BF16) for gm in grams]
        tinvs = [eye2 + gm[:ch, :pair] for gm in grams]
        npows = [gm[:ch, :pair].astype(BF16) for gm in grams]
        steps = 1
        while 2 * steps < ch:
            npows = [_dot(pw, block_diag(pw)).astype(BF16) for pw in npows]
            yield
            tinvs = [ti + _dot(ti.astype(BF16), block_diag(pw)) for ti, pw in zip(tinvs, npows)]
            yield
            steps *= 2
        ws = [_dot(ak, block_diag(blk(v_b, c, g))) for ak, (c, g) in zip(a_ak, prs)]
        gvs = [diag_blocks(_dot_tn(blk(v_b, c, g), blk(k_h, c, g))) for c, g in prs]
        for j, pr in enumerate(prs):
            ready[pr] = (g_bot[j], tinvs[j].astype(BF16), ws[j], gvs[j])

    states = [[state_ref[b, g] for g in pairs] for b in range(nbat)]

    def dependent(chunks):
        sts = states[chunks[0] // per_seq]
        for c in chunks:
            g_bot, tinv, w, gv = zip(*[ready[(c, g)] for g in pairs])
            stb = [block_diag(st.astype(BF16)) for st in sts]
            rhs = [_dot_nt(blk(a_t, c, g), stb[g]) + w[g] for g in pairs]
            yield
            us = [_dot(tinv[g], block_diag(rhs[g].astype(BF16))).astype(BF16) for g in pairs]
            yield
            ys = [_dot_nt(blk(r_t, c, g), stb[g])
                  + _dot(g_bot[g], jnp.concatenate([block_diag(us[g]), block_diag(blk(v_b, c, g))], axis=0))
                  for g in pairs]
            for g in pairs:
                sts[g] = (sts[g] * decay_end[c][:, groups2[g]]
                          + diag_blocks(_dot_tn(us[g], blk(b_h, c, g))) + gv[g])
                y_ref[rows[c], groups2[g]] = ys[g]
            yield

    waves = [[list(range(b * per_seq + c, b * per_seq + min(c + RWKV_GROUP, per_seq))) for b in range(nbat)]
             for c in range(0, per_seq, RWKV_GROUP)]
    _interleave(*[independent(grp) for grp in waves[0]])
    for done, coming in zip(waves, waves[1:]):
        _interleave(*([dependent(grp) for grp in done] + [independent(grp) for grp in coming]))
    _interleave(*[dependent(grp) for grp in waves[-1]])
    for b in range(nbat):
        for g in pairs:
            state_ref[b, g] = states[b][g]

    y = y_ref[...]
    mean = head_sum(y) * (1.0 / HEAD_DIM)
    yc = y - mean
    var = head_sum(yc * yc) * (1.0 / HEAD_DIM)
    y = yc * lax.rsqrt(var + RWKV_LN_EPS) * lnw_ref[...] + lnb_ref[...]
    y = y + head_sum(r * k * rk_ref[...]) * v
    y = (y * out_gate).astype(o_ref.dtype)
    for b in range(nbat):
        o_ref[b] = y[b * t:(b + 1) * t]


def _pad_rows(w, start, total):
    return jnp.zeros((total, w.shape[1]), w.dtype).at[start:start + w.shape[0]].set(w)


def rwkv7(pc, mu, w0, w2, a0, a2, g2, k_k, k_a, r_k, ln_w, ln_b, tile):
    b, s, width = pc.shape
    assert RWKV_CHUNK == HEAD_DIM, "the side-by-side head layout needs square per-head chunk matrices"
    low = width - 3 * GROUP
    dr, ir = w2.shape[0], a2.shape[0]
    nbat = RWKV_SEQS if b % RWKV_SEQS == 0 else 1
    row = lambda bi, i: (bi, i, 0)
    vec = lambda a: a.reshape(1, -1)
    return pl.pallas_call(
        _rwkv_kernel,
        grid=(b // nbat, s // tile),
        in_specs=[
            pl.BlockSpec((nbat, tile, width), row),
            _const_spec((1, width)),
            _const_spec((1, GROUP)),
            _const_spec((low, GROUP)),
            _const_spec((1, GROUP)),
            _const_spec((low, GROUP)),
            _const_spec((low, GROUP)),
            _const_spec((1, GROUP)),
            _const_spec((1, GROUP)),
            _const_spec((1, GROUP)),
            _const_spec((1, GROUP)),
            _const_spec((1, GROUP)),
        ],
        out_specs=pl.BlockSpec((nbat, tile, GROUP), row),
        out_shape=jax.ShapeDtypeStruct((b, s, GROUP), MIX_DTYPE),
        scratch_shapes=[
            pltpu.VMEM((nbat, tile + HALO, width), F32),
            pltpu.VMEM((nbat, N_HEADS // 2, HEAD_DIM, 2 * HEAD_DIM), F32),
            pltpu.VMEM((nbat * tile, GROUP), F32),
        ],
        compiler_params=_params("arbitrary", "arbitrary"),
        name="rwkv7",
    )(pc, vec(mu), vec(w0), _pad_rows(w2, 0, low).astype(BF16), vec(a0),
      _pad_rows(a2, dr, low).astype(BF16), _pad_rows(g2, dr + ir, low).astype(BF16),
      vec(k_k), vec(k_a), vec(r_k), vec(ln_w), vec(ln_b))


def _sb_kernel(q_ref, k_ref, v_ref, o_ref):
    qi = pl.program_id(1)
    blk = q_ref.shape[1]
    rr = lax.broadcasted_iota(jnp.int32, (blk, 2 * blk), 0)
    cc = lax.broadcasted_iota(jnp.int32, (blk, 2 * blk), 1)
    sums = jnp.logical_or(rr >= cc, cc >= blk).astype(BF16)
    causal = (lax.broadcasted_iota(jnp.int32, (blk, blk), 1) < lax.broadcasted_iota(jnp.int32, (blk, blk), 0))
    scale = HEAD_DIM ** -0.5
    qs = [q_ref[0, :, h * HEAD_DIM:(h + 1) * HEAD_DIM] * scale for h in range(N_HEADS)]

    heads = range(N_HEADS)
    lanes = [slice(h * HEAD_DIM, (h + 1) * HEAD_DIM) for h in heads]

    def split_sums(lk):
        hi = lk.astype(BF16)
        lo = (lk - hi.astype(F32)).astype(BF16)
        return _dot(hi, sums) + _dot(lo, sums)

    def block_pass(first_kb, count, diagonal, runs):
        kbs = [first_kb - n for n in range(count)]
        exists = [None if n == 0 else kb >= 0 for n, kb in enumerate(kbs)]
        rows = [pl.ds(pl.multiple_of(jnp.maximum(kb, 0) * blk, blk), blk) for kb in kbs]
        zs = [[_dot_nt(qs[h], k_ref[0, rw, lanes[h]]) for h in heads] for rw in rows]
        lks = [[-_softplus(z) for z in zb] for zb in zs]
        if diagonal:
            lks[0] = [jnp.where(causal, lk, 0.0) for lk in lks[0]]
        for n in range(1, count):
            lks[n] = [jnp.where(exists[n], lk, 0.0) for lk in lks[n]]
        ss = [[split_sums(lk) for lk in lb] for lb in lks]
        logw = []
        for n in range(count):
            if n == 0 and diagonal:
                logw.append([jnp.where(causal, z + s[:, :blk], NEG_BIG) for z, s in zip(zs[0], ss[0])])
                runs = [s[:, blk:] for s in ss[0]]
                continue
            lw = [z + s[:, :blk] + run for z, s, run in zip(zs[n], ss[n], runs)]
            if n > 0:
                lw = [jnp.where(exists[n], x, NEG_BIG) for x in lw]
            logw.append(lw)
            runs = [run + s[:, blk:] for s, run in zip(ss[n], runs)]
        pvs = [functools.reduce(lambda a, b: a + b,
                                [_dot(jnp.exp(logw[n][h]).astype(BF16), v_ref[0, rows[n], lanes[h]])
                                 for n in range(count)]) for h in heads]
        return pvs, runs

    accs, runs = block_pass(qi, SB_FIRST_BLOCKS, True, None)

    def alive(runs):
        top = functools.reduce(jnp.maximum, runs)[:, 0:1]
        return jnp.max(top, axis=0, keepdims=True)[0, 0] > SB_EXIT

    def cond(carry):
        j, live, _, _ = carry
        return jnp.logical_and(qi - SB_FIRST_BLOCKS - SB_LOOP_BLOCKS * j >= 0, live)

    def body(carry):
        j, _, accs, runs = carry
        pvs, runs = block_pass(qi - SB_FIRST_BLOCKS - SB_LOOP_BLOCKS * j, SB_LOOP_BLOCKS, False, runs)
        return j + 1, alive(runs), tuple(a + p for a, p in zip(accs, pvs)), tuple(runs)

    _, _, accs, _ = lax.while_loop(cond, body, (0, alive(runs), tuple(accs), tuple(runs)))
    o_ref[0] = jnp.concatenate(accs, axis=-1).astype(o_ref.dtype)


def stick_breaking(pd_bf16, blk):
    b, s, _ = pd_bf16.shape
    return pl.pallas_call(
        _sb_kernel,
        grid=(b, s // blk),
        in_specs=[
            pl.BlockSpec((1, blk, GROUP), lambda bi, i: (bi, i, 0)),
            pl.BlockSpec((1, s, GROUP), lambda bi, i: (bi, 0, 1)),
            pl.BlockSpec((1, s, GROUP), lambda bi, i: (bi, 0, 2)),
        ],
        out_specs=pl.BlockSpec((1, blk, GROUP), lambda bi, i: (bi, i, 0)),
        out_shape=jax.ShapeDtypeStruct((b, s, GROUP), MIX_DTYPE),
        compiler_params=_params("arbitrary", "arbitrary"),
        name="stick_breaking",
    )(pd_bf16, pd_bf16, pd_bf16)


def _outproj_kernel(x_ref, ya_ref, yb_ref, yc_ref, yd_ref, gt_ref, w_ref, o_ref):
    y = jnp.concatenate([ya_ref[0], yb_ref[0], yc_ref[0], yd_ref[0]], axis=-1)
    o_ref[0] = x_ref[0] + gt_ref[0] * _dot(y, w_ref[...])


def out_projection(x, ys, gt, w_bf16, tile):
    b, s, d = x.shape
    row = lambda bi, i: (bi, i, 0)
    return pl.pallas_call(
        _outproj_kernel,
        grid=(b, s // tile),
        in_specs=[pl.BlockSpec((1, tile, d), row)]
        + [pl.BlockSpec((1, tile, GROUP), row) for _ in ys]
        + [pl.BlockSpec((1, 1, d), lambda bi, i: (bi, 0, 0)), _const_spec(w_bf16.shape)],
        out_specs=pl.BlockSpec((1, tile, d), row),
        out_shape=jax.ShapeDtypeStruct((b, s, d), F32),
        compiler_params=_params("arbitrary", "arbitrary"),
        name="out_projection",
    )(x, *ys, gt, w_bf16)


def _ffn_kernel(x_ref, xh_ref, sc_ref, sh_ref, gt_ref, g_ref, wv_ref, wg_ref, cwv_ref, cwg_ref,
                cbv_ref, cbg_ref, wd_ref, o_ref, uv_ref, ug_ref, act_ref):
    i = pl.program_id(1)
    t = x_ref.shape[1]
    x = x_ref[0]
    xe = jnp.concatenate([xh_ref[0], x], axis=0)
    h = _modulated_norm(xe, g_ref[...], sc_ref[0], sh_ref[0]).astype(BF16)
    first = i == 0

    def up(f):
        for u_ref, w_ref in ((uv_ref, wv_ref), (ug_ref, wg_ref)):
            u = _dot(h, w_ref[f])
            u_ref[f % 2, 0:HALO, :] = jnp.where(first, 0.0, u[0:HALO])
            u_ref[f % 2, HALO:, :] = u[HALO:]

    def conv(u_ref, f, cw, cb):
        out = cb
        for j in range(FFN_CONV):
            out = out + cw[j:j + 1, :] * u_ref[f % 2, pl.ds(HALO - (FFN_CONV - 1) + j, t), :]
        return out

    nf = wv_ref.shape[0]
    fc = wv_ref.shape[2]
    up(0)
    for f in range(nf):
        if f + 1 < nf:
            up(f + 1)
        val = conv(uv_ref, f, cwv_ref[f], cbv_ref[f])
        gate = conv(ug_ref, f, cwg_ref[f], cbg_ref[f])
        act_ref[:, f * fc:(f + 1) * fc] = (val * _gelu_tanh(gate)).astype(BF16)
    o_ref[0] = x + gt_ref[0] * _dot(act_ref[...], wd_ref[...])


def conv_ffn(x, sc, sh, gt, g, w_up, conv_w, conv_b, w_down, tile, fchunk):
    b, s, d = x.shape
    f = w_down.shape[0]
    nf = f // fchunk
    row = lambda bi, i: (bi, i, 0)
    vec = lambda bi, i: (bi, 0, 0)
    halo = lambda bi, i: (bi, jnp.maximum(i * (tile // HALO) - 1, 0), 0)
    cols = lambda w: w.reshape(w.shape[0], nf, fchunk).transpose(1, 0, 2)
    wv = cols(w_up[:, :f]).astype(BF16)
    wg = cols(w_up[:, f:]).astype(BF16)
    wd = w_down.astype(BF16)
    cwv, cwg = cols(conv_w[:, :f]), cols(conv_w[:, f:])
    cbv, cbg = cols(conv_b[None, :f]), cols(conv_b[None, f:])
    return pl.pallas_call(
        _ffn_kernel,
        grid=(b, s // tile),
        in_specs=[
            pl.BlockSpec((1, tile, d), row),
            pl.BlockSpec((1, HALO, d), halo),
            pl.BlockSpec((1, 1, d), vec),
            pl.BlockSpec((1, 1, d), vec),
            pl.BlockSpec((1, 1, d), vec),
            _const_spec((1, d)),
            _const_spec(wv.shape),
            _const_spec(wg.shape),
            _const_spec(cwv.shape),
            _const_spec(cwg.shape),
            _const_spec(cbv.shape),
            _const_spec(cbg.shape),
            _const_spec(wd.shape),
        ],
        out_specs=pl.BlockSpec((1, tile, d), row),
        out_shape=jax.ShapeDtypeStruct((b, s, d), F32),
        scratch_shapes=[
            pltpu.VMEM((2, tile + HALO, fchunk), F32),
            pltpu.VMEM((2, tile + HALO, fchunk), F32),
            pltpu.VMEM((tile, f), BF16),
        ],
        compiler_params=_params("arbitrary", "arbitrary"),
        name="conv_ffn",
    )(x, x, sc, sh, gt, g.reshape(1, d), wv, wg, cwv, cwg, cbv, cbg, wd)


def kernel(x, c, ada_w, ada_b, norm1_g, norm2_g, w_in, w_out, attn_q_gain, attn_k_gain, attn_rel_bias, lru_conv_w, lru_conv_b, lru_ra_w, lru_ra_b, lru_ri_w, lru_ri_b, lru_lambda, rwkv_mu, rwkv_w0, rwkv_w2, rwkv_a0, rwkv_a2, rwkv_g2, rwkv_k_k, rwkv_k_a, rwkv_r_k, rwkv_ln_w, rwkv_ln_b, ffn_up, ffn_conv_w, ffn_conv_b, ffn_down):
    depth = ada_w.shape[0]
    bsz, seq, d = x.shape
    p_rwkv = rwkv_mu.shape[1]
    widths = (3 * GROUP, 2 * GROUP, p_rwkv, 3 * GROUP)
    tile = min(SEQ_TILE, seq)
    mtile = min(MATMUL_TILE, seq)
    qb = min(ATTN_QB, seq)
    mods = ada_modulation(c, ada_w, ada_b)
    for l in range(depth):
        sh_m, sc_m, gt_m, sh_f, sc_f, gt_f = [
            mods[l, :, j * d:(j + 1) * d].reshape(bsz, 1, d) for j in range(6)]
        pa, pb, pc, pd = in_projection(x, sc_m, sh_m, norm1_g[l], w_in[l].astype(BF16),
                                       attn_q_gain[l], attn_k_gain[l], widths, mtile)
        y_a = chunk_attention(pa, attention_bias(attn_rel_bias[l], qb), qb)
        y_b = rglru(pb, lru_conv_w[l], lru_conv_b[l], lru_ra_w[l], lru_ra_b[l],
                    lru_ri_w[l], lru_ri_b[l], lru_lambda[l], tile)
        y_c = rwkv7(pc, rwkv_mu[l], rwkv_w0[l], rwkv_w2[l], rwkv_a0[l], rwkv_a2[l], rwkv_g2[l],
                    rwkv_k_k[l], rwkv_k_a[l], rwkv_r_k[l], rwkv_ln_w[l], rwkv_ln_b[l], tile)
        y_d = stick_breaking(pd, min(SB_BLOCK, seq))
        x = out_projection(x, (y_a, y_b, y_c, y_d), gt_m, w_out[l].astype(BF16), mtile)
        x = conv_ffn(x, sc_f, sh_f, gt_f, norm2_g[l], ffn_up[l], ffn_conv_w[l], ffn_conv_b[l],
                     ffn_down[l], mtile, FFN_CHUNK)
    return x
```

```python
import functools

import jax
import jax.numpy as jnp
from jax import lax
from jax.experimental import pallas as pl
from jax.experimental.pallas import tpu as pltpu

F32 = jnp.float32
BF16 = jnp.bfloat16
MIX_DTYPE = BF16
HI = lax.Precision.HIGHEST

HEAD_DIM = 64
N_HEADS = 4
GROUP = HEAD_DIM * N_HEADS
CHUNK = 64
ATTN_LEFT_CHUNKS = 8
ATTN_MAX_REL = 128
LRU_CONV = 4
LRU_C = 8.0
RWKV_LN_EPS = 64e-5
FFN_CONV = 3
NORM_EPS = 1e-6
HALO = 8
NEG_BIG = -1e30
SB_EXIT = -104.0

VMEM_LIMIT = 56 * 1024 * 1024

ATTN_QB = 256
SEQ_TILE = 512
MATMUL_TILE = 1024
SB_BLOCK = 128
SB_FIRST_BLOCKS = 3
SB_LOOP_BLOCKS = 2
RWKV_CHUNK = 64
RWKV_GROUP = 4
RWKV_SEQS = 4
FFN_CHUNK = 256


def _params(*sem):
    return pltpu.CompilerParams(dimension_semantics=sem, vmem_limit_bytes=VMEM_LIMIT)


def _sigmoid(x):
    return 1.0 / (1.0 + jnp.exp(-x))


def _softplus(x):
    return jnp.maximum(x, 0.0) + jnp.log(1.0 + jnp.exp(-jnp.abs(x)))


def _gelu_tanh(x):
    return 0.5 * x * (1.0 + jnp.tanh(0.7978845608028654 * (x + 0.044715 * x * x * x)))


def _dot_nt(a, b, **kw):
    return lax.dot_general(a, b, (((1,), (1,)), ((), ())), preferred_element_type=F32, **kw)


def _dot_tn(a, b, **kw):
    return lax.dot_general(a, b, (((0,), (0,)), ((), ())), preferred_element_type=F32, **kw)


def _dot(a, b, **kw):
    return jnp.dot(a, b, preferred_element_type=F32, **kw)


def _interleave(*stage_generators):
    live = list(stage_generators)
    done = object()
    while live:
        for gen in list(live):
            if next(gen, done) is done:
                live.remove(gen)


def _const_spec(shape):
    nd = len(shape)
    return pl.BlockSpec(shape, lambda *_: (0,) * nd, pipeline_mode=pl.Buffered(1))


def _ada_kernel(c_ref, w_ref, b_ref, o_ref):
    c = c_ref[...]
    s = c * _sigmoid(c)
    o_ref[0] = _dot(s, w_ref[0], precision=HI) + b_ref[0]


def ada_modulation(c, ada_w, ada_b):
    nl, d, n = ada_w.shape
    b = c.shape[0]
    rows = 8
    cp = jnp.zeros((rows, d), F32).at[:b].set(c)
    tn = 1536
    out = pl.pallas_call(
        _ada_kernel,
        grid=(nl, n // tn),
        in_specs=[
            pl.BlockSpec((rows, d), lambda l, j: (0, 0)),
            pl.BlockSpec((1, d, tn), lambda l, j: (l, 0, j)),
            pl.BlockSpec((1, 1, tn), lambda l, j: (l, 0, j)),
        ],
        out_specs=pl.BlockSpec((1, rows, tn), lambda l, j: (l, 0, j)),
        out_shape=jax.ShapeDtypeStruct((nl, rows, n), F32),
        compiler_params=_params("arbitrary", "arbitrary"),
        name="ada_modulation",
    )(cp, ada_w, ada_b.reshape(nl, 1, n))
    return out[:, :b]


def _modulated_norm(x, g, sc, sh):
    ms = jnp.mean(x * x, axis=-1, keepdims=True)
    return (x * lax.rsqrt(ms + NORM_EPS) * g) * (1.0 + sc) + sh


def _inproj_kernel(x_ref, sc_ref, sh_ref, g_ref, w_ref, qkg_ref, oa_ref, ob_ref, oc_ref, od_ref, *, cuts):
    h = _modulated_norm(x_ref[0], g_ref[...], sc_ref[0], sh_ref[0]).astype(BF16)
    c0, c1, c2, c3 = cuts
    pa = _dot(h, w_ref[:, 0:c0])
    ob_ref[0] = _dot(h, w_ref[:, c0:c1])
    oc_ref[0] = _dot(h, w_ref[:, c1:c2])
    od_ref[0] = _dot(h, w_ref[:, c2:c3]).astype(BF16)
    nqk = 2 * GROUP
    pair = 2 * HEAD_DIM
    hr = lax.broadcasted_iota(jnp.int32, (pair, pair), 0) // HEAD_DIM
    hc = lax.broadcasted_iota(jnp.int32, (pair, pair), 1) // HEAD_DIM
    same_head = (hr == hc).astype(BF16)
    for j in range(0, nqk, pair):
        qk = pa[:, j:j + pair]
        sq = qk * qk
        hi = sq.astype(BF16)
        lo = (sq - hi.astype(F32)).astype(BF16)
        ms = (_dot(hi, same_head) + _dot(lo, same_head)) * (1.0 / HEAD_DIM)
        oa_ref[0, :, j:j + pair] = (qk * lax.rsqrt(ms + NORM_EPS) * qkg_ref[:, j:j + pair]).astype(BF16)
    oa_ref[0, :, nqk:] = pa[:, nqk:].astype(BF16)


def in_projection(x, sc, sh, g, w_bf16, q_gain, k_gain, widths, tile):
    b, s, d = x.shape
    cuts = tuple(int(sum(widths[:i + 1])) for i in range(4))
    row = lambda bi, i: (bi, i, 0)
    vec = lambda bi, i: (bi, 0, 0)
    dts = (BF16, F32, F32, BF16)
    qkg = jnp.concatenate([jnp.tile(q_gain.astype(F32), N_HEADS) * (HEAD_DIM ** -0.5),
                           jnp.tile(k_gain.astype(F32), N_HEADS)]).reshape(1, 2 * GROUP)
    return pl.pallas_call(
        functools.partial(_inproj_kernel, cuts=cuts),
        grid=(b, s // tile),
        in_specs=[
            pl.BlockSpec((1, tile, d), row),
            pl.BlockSpec((1, 1, d), vec),
            pl.BlockSpec((1, 1, d), vec),
            _const_spec((1, d)),
            _const_spec(w_bf16.shape),
            _const_spec((1, 2 * GROUP)),
        ],
        out_specs=[pl.BlockSpec((1, tile, w), row) for w in widths],
        out_shape=[jax.ShapeDtypeStruct((b, s, w), dt) for w, dt in zip(widths, dts)],
        compiler_params=_params("arbitrary", "arbitrary"),
        name="in_projection",
    )(x, sc, sh, g.reshape(1, d), w_bf16, qkg)


def _attn_kernel(*refs, nb):
    q_ref, k_refs, v_refs = refs[0], refs[1:1 + nb], refs[1 + nb:1 + 2 * nb]
    bias_ref, o_ref = refs[1 + 2 * nb:]
    heads = range(N_HEADS)
    lane = lax.broadcasted_iota(jnp.int32, (1, 2 * HEAD_DIM), 1)
    keep = [lane < HEAD_DIM, lane >= HEAD_DIM]
    groups = [slice(g * 2 * HEAD_DIM, (g + 1) * 2 * HEAD_DIM) for g in range(N_HEADS // 2)]
    q2 = [q_ref[0, :, g] for g in groups]
    k2 = [jnp.concatenate([r[0, :, g] for r in k_refs], axis=0) for g in groups]
    v2 = [jnp.concatenate([r[0, :, g] for r in v_refs], axis=0) for g in groups]
    qs = [jnp.where(keep[h % 2], q2[h // 2], jnp.zeros_like(q2[0])) for h in heads]
    ss = [_dot_nt(qs[h], k2[h // 2]) + bias_ref[0, h] for h in heads]
    ps = [jnp.exp(s - jnp.max(s, axis=-1, keepdims=True)) for s in ss]
    dens = [jnp.sum(p, axis=-1, keepdims=True) for p in ps]
    outs = [_dot(ps[h].astype(BF16), v2[h // 2]) / dens[h] for h in heads]
    for g, lanes in enumerate(groups):
        o_ref[0, :, lanes] = jnp.where(keep[0], outs[2 * g], outs[2 * g + 1]).astype(o_ref.dtype)


def attention_bias(rel_bias, qb):
    nh = rel_bias.shape[0]
    nb = 1 + ATTN_LEFT_CHUNKS * CHUNK // qb
    nk = nb * qb
    back = nk - qb
    qi = jnp.arange(qb)[:, None]
    kj = jnp.arange(nk)[None, :]
    qc = (qi + back) // CHUNK
    kc = kj // CHUNK
    valid = jnp.logical_and(kc <= qc, kc >= qc - ATTN_LEFT_CHUNKS)
    m = jnp.arange(nk + 1)
    rel = jnp.where(m < back + CHUNK,
                    jnp.clip(back - m, -ATTN_MAX_REL, ATTN_MAX_REL) + ATTN_MAX_REL, 2 * ATTN_MAX_REL)
    vec = rel_bias.astype(F32)[:, rel]
    toeplitz = jnp.tile(vec, (1, qb))[:, :qb * nk].reshape(nh, qb, nk)
    first_key = (nb - 1 - jnp.arange(nb))[:, None, None, None] * qb
    live = jnp.logical_and(valid[None, None], kj[None, None] >= first_key)
    return jnp.where(live, toeplitz[None], NEG_BIG)


def chunk_attention(pa, bias, qb):
    b, s, _ = pa.shape
    nb = bias.shape[0]
    blk = (1, qb, GROUP)

    def window(col):
        return [pl.BlockSpec(blk, functools.partial(
            lambda bi, i, back: (bi, jnp.maximum(i - back, 0), col), back=back))
            for back in range(nb - 1, -1, -1)]

    return pl.pallas_call(
        functools.partial(_attn_kernel, nb=nb),
        grid=(b, s // qb),
        in_specs=[pl.BlockSpec(blk, lambda bi, i: (bi, i, 0))] + window(1) + window(2) + [
            pl.BlockSpec((1,) + bias.shape[1:], lambda bi, i: (jnp.minimum(i, nb - 1), 0, 0, 0)),
        ],
        out_specs=pl.BlockSpec(blk, lambda bi, i: (bi, i, 0)),
        out_shape=jax.ShapeDtypeStruct((b, s, GROUP), MIX_DTYPE),
        compiler_params=_params("arbitrary", "arbitrary"),
        name="chunk_attention",
    )(*([pa] * (1 + 2 * nb)), bias)


def _lru_kernel(p_ref, cw_ref, cb_ref, raw_ref, rab_ref, riw_ref, rib_ref, lam_ref, o_ref,
                xpad_ref, h_ref):
    i = pl.program_id(1)
    t = p_ref.shape[1]

    @pl.when(i == 0)
    def _():
        xpad_ref[0:HALO, :] = jnp.zeros((HALO, GROUP), F32)
        h_ref[...] = jnp.zeros_like(h_ref)

    @pl.when(i > 0)
    def _():
        xpad_ref[0:HALO, :] = xpad_ref[t:t + HALO, :]

    xpad_ref[HALO:HALO + t, :] = p_ref[0, :, 0:GROUP]
    gate = p_ref[0, :, GROUP:2 * GROUP]
    xc = cb_ref[...]
    for j in range(LRU_CONV):
        xc = xc + cw_ref[j:j + 1, :] * xpad_ref[pl.ds(HALO - (LRU_CONV - 1) + j, t), :]
    xcb = xc.astype(BF16)
    r_gate = _sigmoid(_dot(xcb, raw_ref[...]) + rab_ref[...])
    i_gate = _sigmoid(_dot(xcb, riw_ref[...]) + rib_ref[...])
    log_a = (-LRU_C * r_gate) * _softplus(-lam_ref[...])
    a = jnp.exp(log_a)
    u = jnp.sqrt(-jnp.tanh(log_a) * (a * a + 1.0)) * (i_gate * xc)
    rows = lax.broadcasted_iota(jnp.int32, (t, 1), 0)
    k = 1
    while k < t:
        keep = rows >= k
        a_sh = jnp.where(keep, pltpu.roll(a, k, 0), 1.0)
        u_sh = jnp.where(keep, pltpu.roll(u, k, 0), 0.0)
        u = a * u_sh + u
        a = a * a_sh
        k *= 2
    hh = a * h_ref[...] + u
    h_ref[...] = hh[t - 1:t, :]
    o_ref[0] = (hh * _gelu_tanh(gate)).astype(o_ref.dtype)


def _block_diag(w):
    h, n, _ = w.shape
    eye = jnp.eye(h, dtype=w.dtype)
    return (eye[:, None, :, None] * w[:, :, None, :]).reshape(h * n, h * n)


def rglru(pb, conv_w, conv_b, ra_w, ra_b, ri_w, ri_b, lam, tile):
    b, s, _ = pb.shape
    row = lambda bi, i: (bi, i, 0)
    vec = lambda a: a.reshape(1, GROUP)
    return pl.pallas_call(
        _lru_kernel,
        grid=(b, s // tile),
        in_specs=[
            pl.BlockSpec((1, tile, 2 * GROUP), row),
            _const_spec((LRU_CONV, GROUP)),
            _const_spec((1, GROUP)),
            _const_spec((GROUP, GROUP)),
            _const_spec((1, GROUP)),
            _const_spec((GROUP, GROUP)),
            _const_spec((1, GROUP)),
            _const_spec((1, GROUP)),
        ],
        out_specs=pl.BlockSpec((1, tile, GROUP), row),
        out_shape=jax.ShapeDtypeStruct((b, s, GROUP), MIX_DTYPE),
        scratch_shapes=[pltpu.VMEM((tile + HALO, GROUP), F32), pltpu.VMEM((1, GROUP), F32)],
        compiler_params=_params("arbitrary", "arbitrary"),
        name="rglru",
    )(pb, conv_w, vec(conv_b), _block_diag(ra_w).astype(BF16), vec(ra_b),
      _block_diag(ri_w).astype(BF16), vec(ri_b), vec(lam))


def _rwkv_kernel(p_ref, mu_ref, w0_ref, w2_ref, a0_ref, a2_ref, g2_ref, kk_ref, ka_ref, rk_ref,
                 lnw_ref, lnb_ref, o_ref, ppad_ref, state_ref, y_ref):
    i = pl.program_id(1)
    nbat, t, width = p_ref.shape
    ch = RWKV_CHUNK
    g3 = 3 * GROUP

    @pl.when(i == 0)
    def _():
        ppad_ref[:, 0:HALO, :] = jnp.zeros((nbat, HALO, width), F32)
        state_ref[...] = jnp.zeros_like(state_ref)

    @pl.when(i > 0)
    def _():
        ppad_ref[:, 0:HALO, :] = ppad_ref[:, t:t + HALO, :]

    ppad_ref[:, HALO:HALO + t, :] = p_ref[...]
    p = jnp.concatenate([p_ref[b] for b in range(nbat)], axis=0)
    p_prev = jnp.concatenate([ppad_ref[b, pl.ds(HALO - 1, t), :] for b in range(nbat)], axis=0)
    p = p + (p_prev - p) * mu_ref[...]
    r = p[:, 0:GROUP]
    k = p[:, GROUP:2 * GROUP]
    v = p[:, 2 * GROUP:g3]
    low = p[:, g3:width]
    w = -_softplus(-(w0_ref[...] + _dot(jnp.tanh(low).astype(BF16), w2_ref[...]))) - 0.5
    a = _sigmoid(a0_ref[...] + _dot(low.astype(BF16), a2_ref[...]))
    out_gate = _dot(_sigmoid(low).astype(BF16), g2_ref[...])

    hr = lax.broadcasted_iota(jnp.int32, (GROUP, GROUP), 0) // HEAD_DIM
    hc = lax.broadcasted_iota(jnp.int32, (GROUP, GROUP), 1) // HEAD_DIM
    head_ones = (hr == hc).astype(BF16)

    def split(x):
        hi = x.astype(BF16)
        return hi, (x - hi.astype(F32)).astype(BF16)

    def head_sum(x):
        hi, lo = split(x)
        return _dot(hi, head_ones) + _dot(lo, head_ones)

    kk = k * kk_ref[...]
    kk = kk / jnp.maximum(jnp.sqrt(head_sum(kk * kk)), 1e-12)
    k = k * (1.0 + (a - 1.0) * ka_ref[...])

    logd = -jnp.exp(w)
    kka = kk * a
    per_seq = t // ch
    nch = nbat * per_seq
    rows = [slice(c * ch, (c + 1) * ch) for c in range(nch)]
    pair = 2 * HEAD_DIM
    groups2 = [slice(g * pair, (g + 1) * pair) for g in range(N_HEADS // 2)]
    lane = lax.broadcasted_iota(jnp.int32, (1, pair), 1)
    first_head = lane < HEAD_DIM

    def blk(x, c, g):
        return x[rows[c], groups2[g]]

    def block_diag(x2):
        zero = jnp.zeros_like(x2)
        return jnp.concatenate([jnp.where(first_head, x2, zero), jnp.where(first_head, zero, x2)], axis=0)

    def diag_blocks(full):
        return jnp.where(first_head, full[:HEAD_DIM], full[HEAD_DIM:])

    rr = lax.broadcasted_iota(jnp.int32, (ch, ch), 0)
    cc = lax.broadcasted_iota(jnp.int32, (ch, ch), 1)
    tril = (rr >= cc).astype(BF16)
    er = lax.broadcasted_iota(jnp.int32, (ch, pair), 0)
    ec = jnp.bitwise_and(lax.broadcasted_iota(jnp.int32, (ch, pair), 1), ch - 1)
    eye2 = (er == ec).astype(F32)
    gr = lax.broadcasted_iota(jnp.int32, (2 * ch, 2 * pair), 0)
    gc = jnp.bitwise_and(lax.broadcasted_iota(jnp.int32, (2 * ch, 2 * pair), 1), ch - 1)
    gram_mask = jnp.logical_or(jnp.logical_and(gr < ch, gc < gr),
                               jnp.logical_and(gr >= ch, gc <= gr - ch))

    logd_hi, logd_lo = split(logd)
    cums = [_dot(tril, logd_hi[rs, :]) + _dot(tril, logd_lo[rs, :]) for rs in rows]
    cum = jnp.concatenate(cums, axis=0)
    cum_end = jnp.concatenate([jnp.broadcast_to(cm[ch - 1:ch, :], (ch, GROUP)) for cm in cums], axis=0)
    p_inv = jnp.exp(-cum)
    p_end = jnp.exp(cum_end - cum)
    a_t = (-kk * jnp.exp(cum - logd)).astype(BF16)
    b_t = (kka * p_inv).astype(BF16)
    k_t = (k * p_inv).astype(BF16)
    r_t = (r * jnp.exp(cum)).astype(BF16)
    b_h = (kka * p_end).astype(BF16)
    k_h = (k * p_end).astype(BF16)
    v_b = v.astype(BF16)
    decay_end = [jnp.exp(cm[ch - 1:ch, :]) for cm in cums]
    pairs = range(len(groups2))
    ready = {}

    def independent(chunks):
        prs = [(c, g) for c in chunks for g in pairs]
        grams = [jnp.where(gram_mask,
                           _dot_nt(jnp.concatenate([blk(a_t, c, g), blk(r_t, c, g)], axis=0),
                                   jnp.concatenate([block_diag(blk(b_t, c, g)), block_diag(blk(k_t, c, g))],
                                                   axis=0)), 0.0)
                 for c, g in prs]
        yield
        a_ak = [gm[:ch, pair:].astype(BF16) for gm in grams]
        g_bot = [gm[ch:].astype(BF16) for gm in grams]
        tinvs = [eye2 + gm[:ch, :pair] for gm in grams]
        npows = [gm[:ch, :pair].astype(BF16) for gm in grams]
        steps = 1
        while 2 * steps < ch:
            npows = [_dot(pw, block_diag(pw)).astype(BF16) for pw in npows]
            yield
            tinvs = [ti + _dot(ti.astype(BF16), block_diag(pw)) for ti, pw in zip(tinvs, npows)]
            yield
            steps *= 2
        ws = [_dot(ak, block_diag(blk(v_b, c, g))) for ak, (c, g) in zip(a_ak, prs)]
        gvs = [diag_blocks(_dot_tn(blk(v_b, c, g), blk(k_h, c, g))) for c, g in prs]
        for j, pr in enumerate(prs):
            ready[pr] = (g_bot[j], tinvs[j].astype(BF16), ws[j], gvs[j])

    states = [[state_ref[b, g] for g in pairs] for b in range(nbat)]

    def dependent(chunks):
        sts = states[chunks[0] // per_seq]
        for c in chunks:
            g_bot, tinv, w, gv = zip(*[ready[(c, g)] for g in pairs])
            stb = [block_diag(st.astype(BF16)) for st in sts]
            rhs = [_dot_nt(blk(a_t, c, g), stb[g]) + w[g] for g in pairs]
            yield
            us = [_dot(tinv[g], block_diag(rhs[g].astype(BF16))).astype(BF16) for g in pairs]
            yield
            ys = [_dot_nt(blk(r_t, c, g), stb[g])
                  + _dot(g_bot[g], jnp.concatenate([block_diag(us[g]), block_diag(blk(v_b, c, g))], axis=0))
                  for g in pairs]
            for g in pairs:
                sts[g] = (sts[g] * decay_end[c][:, groups2[g]]
                          + diag_blocks(_dot_tn(us[g], blk(b_h, c, g))) + gv[g])
                y_ref[rows[c], groups2[g]] = ys[g]
            yield

    waves = [[list(range(b * per_seq + c, b * per_seq + min(c + RWKV_GROUP, per_seq))) for b in range(nbat)]
             for c in range(0, per_seq, RWKV_GROUP)]
    _interleave(*[independent(grp) for grp in waves[0]])
    for done, coming in zip(waves, waves[1:]):
        _interleave(*([dependent(grp) for grp in done] + [independent(grp) for grp in coming]))
    _interleave(*[dependent(grp) for grp in waves[-1]])
    for b in range(nbat):
        for g in pairs:
            state_ref[b, g] = states[b][g]

    y = y_ref[...]
    mean = head_sum(y) * (1.0 / HEAD_DIM)
    yc = y - mean
    var = head_sum(yc * yc) * (1.0 / HEAD_DIM)
    y = yc * lax.rsqrt(var + RWKV_LN_EPS) * lnw_ref[...] + lnb_ref[...]
    y = y + head_sum(r * k * rk_ref[...]) * v
    y = (y * out_gate).astype(o_ref.dtype)
    for b in range(nbat):
        o_ref[b] = y[b * t:(b + 1) * t]


def _pad_rows(w, start, total):
    return jnp.zeros((total, w.shape[1]), w.dtype).at[start:start + w.shape[0]].set(w)


def rwkv7(pc, mu, w0, w2, a0, a2, g2, k_k, k_a, r_k, ln_w, ln_b, tile):
    b, s, width = pc.shape
    assert RWKV_CHUNK == HEAD_DIM, "the side-by-side head layout needs square per-head chunk matrices"
    low = width - 3 * GROUP
    dr, ir = w2.shape[0], a2.shape[0]
    nbat = RWKV_SEQS if b % RWKV_SEQS == 0 else 1
    row = lambda bi, i: (bi, i, 0)
    vec = lambda a: a.reshape(1, -1)
    return pl.pallas_call(
        _rwkv_kernel,
        grid=(b // nbat, s // tile),
        in_specs=[
            pl.BlockSpec((nbat, tile, width), row),
            _const_spec((1, width)),
            _const_spec((1, GROUP)),
            _const_spec((low, GROUP)),
            _const_spec((1, GROUP)),
            _const_spec((low, GROUP)),
            _const_spec((low, GROUP)),
            _const_spec((1, GROUP)),
            _const_spec((1, GROUP)),
            _const_spec((1, GROUP)),
            _const_spec((1, GROUP)),
            _const_spec((1, GROUP)),
        ],
        out_specs=pl.BlockSpec((nbat, tile, GROUP), row),
        out_shape=jax.ShapeDtypeStruct((b, s, GROUP), MIX_DTYPE),
        scratch_shapes=[
            pltpu.VMEM((nbat, tile + HALO, width), F32),
            pltpu.VMEM((nbat, N_HEADS // 2, HEAD_DIM, 2 * HEAD_DIM), F32),
            pltpu.VMEM((nbat * tile, GROUP), F32),
        ],
        compiler_params=_params("arbitrary", "arbitrary"),
        name="rwkv7",
    )(pc, vec(mu), vec(w0), _pad_rows(w2, 0, low).astype(BF16), vec(a0),
      _pad_rows(a2, dr, low).astype(BF16), _pad_rows(g2, dr + ir, low).astype(BF16),
      vec(k_k), vec(k_a), vec(r_k), vec(ln_w), vec(ln_b))


def _sb_kernel(q_ref, k_ref, v_ref, o_ref):
    qi = pl.program_id(1)
    blk = q_ref.shape[1]
    rr = lax.broadcasted_iota(jnp.int32, (blk, 2 * blk), 0)
    cc = lax.broadcasted_iota(jnp.int32, (blk, 2 * blk), 1)
    sums = jnp.logical_or(rr >= cc, cc >= blk).astype(BF16)
    causal = (lax.broadcasted_iota(jnp.int32, (blk, blk), 1) < lax.broadcasted_iota(jnp.int32, (blk, blk), 0))
    scale = HEAD_DIM ** -0.5
    qs = [q_ref[0, :, h * HEAD_DIM:(h + 1) * HEAD_DIM] * scale for h in range(N_HEADS)]

    heads = range(N_HEADS)
    lanes = [slice(h * HEAD_DIM, (h + 1) * HEAD_DIM) for h in heads]

    def split_sums(lk):
        hi = lk.astype(BF16)
        lo = (lk - hi.astype(F32)).astype(BF16)
        return _dot(hi, sums) + _dot(lo, sums)

    def block_pass(first_kb, count, diagonal, runs):
        kbs = [first_kb - n for n in range(count)]
        exists = [None if n == 0 else kb >= 0 for n, kb in enumerate(kbs)]
        rows = [pl.ds(pl.multiple_of(jnp.maximum(kb, 0) * blk, blk), blk) for kb in kbs]
        zs = [[_dot_nt(qs[h], k_ref[0, rw, lanes[h]]) for h in heads] for rw in rows]
        lks = [[-_softplus(z) for z in zb] for zb in zs]
        if diagonal:
            lks[0] = [jnp.where(causal, lk, 0.0) for lk in lks[0]]
        for n in range(1, count):
            lks[n] = [jnp.where(exists[n], lk, 0.0) for lk in lks[n]]
        ss = [[split_sums(lk) for lk in lb] for lb in lks]
        logw = []
        for n in range(count):
            if n == 0 and diagonal:
                logw.append([jnp.where(causal, z + s[:, :blk], NEG_BIG) for z, s in zip(zs[0], ss[0])])
                runs = [s[:, blk:] for s in ss[0]]
                continue
            lw = [z + s[:, :blk] + run for z, s, run in zip(zs[n], ss[n], runs)]
            if n > 0:
                lw = [jnp.where(exists[n], x, NEG_BIG) for x in lw]
            logw.append(lw)
            runs = [run + s[:, blk:] for s, run in zip(ss[n], runs)]
        pvs = [functools.reduce(lambda a, b: a + b,
                                [_dot(jnp.exp(logw[n][h]).astype(BF16), v_ref[0, rows[n], lanes[h]])
                                 for n in range(count)]) for h in heads]
        return pvs, runs

    accs, runs = block_pass(qi, SB_FIRST_BLOCKS, True, None)

    def alive(runs):
        top = functools.reduce(jnp.maximum, runs)[:, 0:1]
        return jnp.max(top, axis=0, keepdims=True)[0, 0] > SB_EXIT

    def cond(carry):
        j, live, _, _ = carry
        return jnp.logical_and(qi - SB_FIRST_BLOCKS - SB_LOOP_BLOCKS * j >= 0, live)

    def body(carry):
        j, _, accs, runs = carry
        pvs, runs = block_pass(qi - SB_FIRST_BLOCKS - SB_LOOP_BLOCKS * j, SB_LOOP_BLOCKS, False, runs)
        return j + 1, alive(runs), tuple(a + p for a, p in zip(accs, pvs)), tuple(runs)

    _, _, accs, _ = lax.while_loop(cond, body, (0, alive(runs), tuple(accs), tuple(runs)))
    o_ref[0] = jnp.concatenate(accs, axis=-1).astype(o_ref.dtype)


def stick_breaking(pd_bf16, blk):
    b, s, _ = pd_bf16.shape
    return pl.pallas_call(
        _sb_kernel,
        grid=(b, s // blk),
        in_specs=[
            pl.BlockSpec((1, blk, GROUP), lambda bi, i: (bi, i, 0)),
            pl.BlockSpec((1, s, GROUP), lambda bi, i: (bi, 0, 1)),
            pl.BlockSpec((1, s, GROUP), lambda bi, i: (bi, 0, 2)),
        ],
        out_specs=pl.BlockSpec((1, blk, GROUP), lambda bi, i: (bi, i, 0)),
        out_shape=jax.ShapeDtypeStruct((b, s, GROUP), MIX_DTYPE),
        compiler_params=_params("arbitrary", "arbitrary"),
        name="stick_breaking",
    )(pd_bf16, pd_bf16, pd_bf16)


def _outproj_kernel(x_ref, ya_ref, yb_ref, yc_ref, yd_ref, gt_ref, w_ref, o_ref):
    y = jnp.concatenate([ya_ref[0], yb_ref[0], yc_ref[0], yd_ref[0]], axis=-1)
    o_ref[0] = x_ref[0] + gt_ref[0] * _dot(y, w_ref[...])


def out_projection(x, ys, gt, w_bf16, tile):
    b, s, d = x.shape
    row = lambda bi, i: (bi, i, 0)
    return pl.pallas_call(
        _outproj_kernel,
        grid=(b, s // tile),
        in_specs=[pl.BlockSpec((1, tile, d), row)]
        + [pl.BlockSpec((1, tile, GROUP), row) for _ in ys]
        + [pl.BlockSpec((1, 1, d), lambda bi, i: (bi, 0, 0)), _const_spec(w_bf16.shape)],
        out_specs=pl.BlockSpec((1, tile, d), row),
        out_shape=jax.ShapeDtypeStruct((b, s, d), F32),
        compiler_params=_params("arbitrary", "arbitrary"),
        name="out_projection",
    )(x, *ys, gt, w_bf16)


def _ffn_kernel(x_ref, xh_ref, sc_ref, sh_ref, gt_ref, g_ref, wv_ref, wg_ref, cwv_ref, cwg_ref,
                cbv_ref, cbg_ref, wd_ref, o_ref, uv_ref, ug_ref, act_ref):
    i = pl.program_id(1)
    t = x_ref.shape[1]
    x = x_ref[0]
    xe = jnp.concatenate([xh_ref[0], x], axis=0)
    h = _modulated_norm(xe, g_ref[...], sc_ref[0], sh_ref[0]).astype(BF16)
    first = i == 0

    def up(f):
        for u_ref, w_ref in ((uv_ref, wv_ref), (ug_ref, wg_ref)):
            u = _dot(h, w_ref[f])
            u_ref[f % 2, 0:HALO, :] = jnp.where(first, 0.0, u[0:HALO])
            u_ref[f % 2, HALO:, :] = u[HALO:]

    def conv(u_ref, f, cw, cb):
        out = cb
        for j in range(FFN_CONV):
            out = out + cw[j:j + 1, :] * u_ref[f % 2, pl.ds(HALO - (FFN_CONV - 1) + j, t), :]
        return out

    nf = wv_ref.shape[0]
    fc = wv_ref.shape[2]
    up(0)
    for f in range(nf):
        if f + 1 < nf:
            up(f + 1)
        val = conv(uv_ref, f, cwv_ref[f], cbv_ref[f])
        gate = conv(ug_ref, f, cwg_ref[f], cbg_ref[f])
        act_ref[:, f * fc:(f + 1) * fc] = (val * _gelu_tanh(gate)).astype(BF16)
    o_ref[0] = x + gt_ref[0] * _dot(act_ref[...], wd_ref[...])


def conv_ffn(x, sc, sh, gt, g, w_up, conv_w, conv_b, w_down, tile, fchunk):
    b, s, d = x.shape
    f = w_down.shape[0]
    nf = f // fchunk
    row = lambda bi, i: (bi, i, 0)
    vec = lambda bi, i: (bi, 0, 0)
    halo = lambda bi, i: (bi, jnp.maximum(i * (tile // HALO) - 1, 0), 0)
    cols = lambda w: w.reshape(w.shape[0], nf, fchunk).transpose(1, 0, 2)
    wv = cols(w_up[:, :f]).astype(BF16)
    wg = cols(w_up[:, f:]).astype(BF16)
    wd = w_down.astype(BF16)
    cwv, cwg = cols(conv_w[:, :f]), cols(conv_w[:, f:])
    cbv, cbg = cols(conv_b[None, :f]), cols(conv_b[None, f:])
    return pl.pallas_call(
        _ffn_kernel,
        grid=(b, s // tile),
        in_specs=[
            pl.BlockSpec((1, tile, d), row),
            pl.BlockSpec((1, HALO, d), halo),
            pl.BlockSpec((1, 1, d), vec),
            pl.BlockSpec((1, 1, d), vec),
            pl.BlockSpec((1, 1, d), vec),
            _const_spec((1, d)),
            _const_spec(wv.shape),
            _const_spec(wg.shape),
            _const_spec(cwv.shape),
            _const_spec(cwg.shape),
            _const_spec(cbv.shape),
            _const_spec(cbg.shape),
            _const_spec(wd.shape),
        ],
        out_specs=pl.BlockSpec((1, tile, d), row),
        out_shape=jax.ShapeDtypeStruct((b, s, d), F32),
        scratch_shapes=[
            pltpu.VMEM((2, tile + HALO, fchunk), F32),
            pltpu.VMEM((2, tile + HALO, fchunk), F32),
            pltpu.VMEM((tile, f), BF16),
        ],
        compiler_params=_params("arbitrary", "arbitrary"),
        name="conv_ffn",
    )(x, x, sc, sh, gt, g.reshape(1, d), wv, wg, cwv, cwg, cbv, cbg, wd)


def kernel(x, c, ada_w, ada_b, norm1_g, norm2_g, w_in, w_out, attn_q_gain, attn_k_gain, attn_rel_bias, lru_conv_w, lru_conv_b, lru_ra_w, lru_ra_b, lru_ri_w, lru_ri_b, lru_lambda, rwkv_mu, rwkv_w0, rwkv_w2, rwkv_a0, rwkv_a2, rwkv_g2, rwkv_k_k, rwkv_k_a, rwkv_r_k, rwkv_ln_w, rwkv_ln_b, ffn_up, ffn_conv_w, ffn_conv_b, ffn_down):
    depth = ada_w.shape[0]
    bsz, seq, d = x.shape
    p_rwkv = rwkv_mu.shape[1]
    widths = (3 * GROUP, 2 * GROUP, p_rwkv, 3 * GROUP)
    tile = min(SEQ_TILE, seq)
    mtile = min(MATMUL_TILE, seq)
    qb = min(ATTN_QB, seq)
    mods = ada_modulation(c, ada_w, ada_b)
    for l in range(depth):
        sh_m, sc_m, gt_m, sh_f, sc_f, gt_f = [
            mods[l, :, j * d:(j + 1) * d].reshape(bsz, 1, d) for j in range(6)]
        pa, pb, pc, pd = in_projection(x, sc_m, sh_m, norm1_g[l], w_in[l].astype(BF16),
                                       attn_q_gain[l], attn_k_gain[l], widths, mtile)
        y_a = chunk_attention(pa, attention_bias(attn_rel_bias[l], qb), qb)
        y_b = rglru(pb, lru_conv_w[l], lru_conv_b[l], lru_ra_w[l], lru_ra_b[l],
                    lru_ri_w[l], lru_ri_b[l], lru_lambda[l], tile)
        y_c = rwkv7(pc, rwkv_mu[l], rwkv_w0[l], rwkv_w2[l], rwkv_a0[l], rwkv_a2[l], rwkv_g2[l],
                    rwkv_k_k[l], rwkv_k_a[l], rwkv_r_k[l], rwkv_ln_w[l], rwkv_ln_b[l], tile)
        y_d = stick_breaking(pd, min(SB_BLOCK, seq))
        x = out_projection(x, (y_a, y_b, y_c, y_d), gt_m, w_out[l].astype(BF16), mtile)
        x = conv_ffn(x, sc_f, sh_f, gt_f, norm2_g[l], ffn_up[l], ffn_conv_w[l], ffn_conv_b[l],
                     ffn_down[l], mtile, FFN_CHUNK)
    return x
```
